```python
import math
import jax, jax.numpy as jnp
from jax import lax
import numpy as np

D_MODEL = 1024
BATCH = 8
SEQ = 2048
DEPTH = 4
DEC_BATCH = 32
DEC_SEQ = 4
PAST_LEN = 16384
PAGE_SIZE = 128

N_A_LAYERS = DEPTH // 2
N_B_LAYERS = DEPTH - N_A_LAYERS
SSM_GROUP = 16
SSM_GROUPS = D_MODEL // SSM_GROUP
SSM_STATE = 64
DT_MIN = 0.001
DT_MAX = 0.1
N_HEADS = 16
QK_NOPE = 64
QK_ROPE = 32
V_HEAD = 64
KV_LORA = 256
Q_LORA = 384
ROPE_THETA = 10000.0
Q_BLOCK = 128
ATTN_SCALE = (QK_NOPE + QK_ROPE) ** -0.5
N_EXPERTS = 32
TOP_K = 4
D_EXPERT = D_MODEL
SWIGLU_LIMIT = 7.0
SWIGLU_ALPHA = 1.702
EXPERT_BLOCK = 128
LN_EPS = 1e-5
RMS_EPS = 1e-6
DEEPNORM_ALPHA = (2 * DEPTH) ** 0.25
DEEPNORM_BETA = (8 * DEPTH) ** -0.25

kernel_name = 'yoco_s5_mla_moe_step'

F32 = jnp.float32


def layer_norm(x, g, b):
    xf = x.astype(F32)
    mu = jnp.mean(xf, axis=-1, keepdims=True)
    var = jnp.mean(jnp.square(xf - mu), axis=-1, keepdims=True)
    return (xf - mu) * lax.rsqrt(var + LN_EPS) * g.astype(F32) + b.astype(F32)


def rms_norm(x, g):
    xf = x.astype(F32)
    return xf * lax.rsqrt(jnp.mean(jnp.square(xf), axis=-1, keepdims=True) + RMS_EPS) * g.astype(F32)


def rope_cos_sin(pos):
    inv = 1.0 / (ROPE_THETA ** (jnp.arange(0, QK_ROPE, 2, dtype=F32) / QK_ROPE))
    ang = pos.astype(F32)[:, None] * inv[None, :]
    return jnp.cos(ang), jnp.sin(ang)


def apply_rope(x, cos, sin):
    x1, x2 = jnp.split(x.astype(F32), 2, axis=-1)
    return jnp.concatenate([x1 * cos - x2 * sin, x1 * sin + x2 * cos], axis=-1)


def _complex_affine_combine(e1, e2):
    a1r, a1i, b1r, b1i = e1
    a2r, a2i, b2r, b2i = e2
    return (a2r * a1r - a2i * a1i,
            a2r * a1i + a2i * a1r,
            a2r * b1r - a2i * b1i + b2r,
            a2r * b1i + a2i * b1r + b2i)


def s5_mixer(u, lam_re, lam_im, log_dt, b_re, b_im, c_re, c_im, d_skip, w_glu, b_glu, h0=None):
    n, l, _ = u.shape
    lr = lam_re.astype(F32)
    li = lam_im.astype(F32)
    dt = jnp.exp(log_dt.astype(F32))[:, None]
    mag = jnp.exp(lr * dt)
    ang = li * dt
    ab_re = mag * jnp.cos(ang)
    ab_im = mag * jnp.sin(ang)
    den = lr * lr + li * li
    f_re = ((ab_re - 1.0) * lr + ab_im * li) / den
    f_im = (ab_im * lr - (ab_re - 1.0) * li) / den
    br = b_re.astype(F32)
    bi = b_im.astype(F32)
    bb_re = f_re[..., None] * br - f_im[..., None] * bi
    bb_im = f_re[..., None] * bi + f_im[..., None] * br
    ug = u.reshape(n, l, SSM_GROUPS, SSM_GROUP).astype(F32)
    bu_re = jnp.einsum('nlgc,gpc->nlgp', ug, bb_re)
    bu_im = jnp.einsum('nlgc,gpc->nlgp', ug, bb_im)
    a_re = jnp.broadcast_to(ab_re, bu_re.shape)
    a_im = jnp.broadcast_to(ab_im, bu_re.shape)
    _, _, h_re, h_im = lax.associative_scan(_complex_affine_combine, (a_re, a_im, bu_re, bu_im), axis=1)
    if h0 is not None:
        steps = jnp.arange(1, l + 1, dtype=F32)[:, None, None]
        pm = jnp.exp(lr * dt * steps)
        pa = li * dt * steps
        p_re = pm * jnp.cos(pa)
        p_im = pm * jnp.sin(pa)
        s_re = h0[0].astype(F32)[:, None]
        s_im = h0[1].astype(F32)[:, None]
        h_re, h_im = (h_re + p_re * s_re - p_im * s_im,
                      h_im + p_re * s_im + p_im * s_re)
    y = (jnp.einsum('nlgp,gcp->nlgc', h_re, c_re.astype(F32))
         - jnp.einsum('nlgp,gcp->nlgc', h_im, c_im.astype(F32)))
    y = y.reshape(n, l, D_MODEL) + d_skip.astype(F32) * u.astype(F32)
    z = jax.nn.gelu(y).astype(u.dtype)
    o = z @ w_glu + b_glu
    out = o[..., :D_MODEL] * jax.nn.sigmoid(o[..., D_MODEL:])
    return out, (h_re[:, -1], h_im[:, -1])


def mla_latent(x, w_kv_a, g_kv, cos, sin):
    kv = x @ w_kv_a
    c = rms_norm(kv[..., :KV_LORA], g_kv).astype(x.dtype)
    kr = apply_rope(kv[..., KV_LORA:], cos, sin).astype(x.dtype)
    return c, kr


def mla_queries(x, w_q_a, g_q, w_q_b, cos, sin):
    n, l, _ = x.shape
    cq = rms_norm(x @ w_q_a, g_q).astype(x.dtype)
    q = (cq @ w_q_b).reshape(n, l, N_HEADS, QK_NOPE + QK_ROPE)
    q_pe = apply_rope(q[..., QK_NOPE:], cos[:, None, :], sin[:, None, :]).astype(q.dtype)
    return q[..., :QK_NOPE], q_pe


def causal_block_attention(q_nope, q_pe, k_nope, k_pe, v):
    n, l, h, _ = q_nope.shape
    nb = l // Q_BLOCK
    qn = q_nope.reshape(n, nb, Q_BLOCK, h, QK_NOPE).swapaxes(0, 1)
    qp = q_pe.reshape(n, nb, Q_BLOCK, h, QK_ROPE).swapaxes(0, 1)
    k_pos = jnp.arange(l)

    def one_block(args):
        i, qn_b, qp_b = args
        s = (jnp.einsum('nqhd,nkhd->nhqk', qn_b, k_nope)
             + jnp.einsum('nqhr,nkr->nhqk', qp_b, k_pe))
        q_pos = i * Q_BLOCK + jnp.arange(Q_BLOCK)
        s = jnp.where(k_pos[None, :] <= q_pos[:, None], s.astype(F32) * ATTN_SCALE, -jnp.inf)
        p = jax.nn.softmax(s, axis=-1)
        return jnp.einsum('nhqk,nkhd->nqhd', p.astype(v.dtype), v)

    o = lax.map(one_block, (jnp.arange(nb), qn, qp))
    return o.swapaxes(0, 1).reshape(n, l, h * V_HEAD)


def latent_attention(q_nope, q_pe, c_all, kr_all, w_uk, w_uv, q_pos, k_pos):
    n, q, h, _ = q_nope.shape
    q_lat = jnp.einsum('nqhd,chd->nqhc', q_nope, w_uk)
    s = (jnp.einsum('nqhc,nkc->nhqk', q_lat, c_all)
         + jnp.einsum('nqhr,nkr->nhqk', q_pe, kr_all))
    s = jnp.where(k_pos[None, :] <= q_pos[:, None], s.astype(F32) * ATTN_SCALE, -jnp.inf)
    p = jax.nn.softmax(s, axis=-1)
    o_lat = jnp.einsum('nhqk,nkc->nqhc', p.astype(c_all.dtype), c_all)
    o = jnp.einsum('nqhc,chd->nqhd', o_lat, w_uv)
    return o.reshape(n, q, h * V_HEAD)


def moe_ffn(x, w_router, b_router, w_gu, b_gu, w_dn, b_dn):
    shape = x.shape
    xt = x.reshape(-1, D_MODEL)
    t = xt.shape[0]
    logits = jnp.dot(xt.astype(F32), w_router.astype(F32)) + b_router.astype(F32)
    top_logit, top_e = lax.top_k(logits, TOP_K)
    gate = jax.nn.softmax(top_logit, axis=-1)
    n_assign = t * TOP_K
    flat_e = top_e.reshape(n_assign)
    order = jnp.argsort(flat_e)
    sorted_e = flat_e[order]
    counts = jnp.bincount(flat_e, length=N_EXPERTS)
    padded = (counts + EXPERT_BLOCK - 1) // EXPERT_BLOCK * EXPERT_BLOCK
    group_start = jnp.cumsum(counts) - counts
    padded_end = jnp.cumsum(padded)
    padded_start = padded_end - padded
    rank = jnp.arange(n_assign) - group_start[sorted_e]
    dest = padded_start[sorted_e] + rank
    n_blocks = -(-n_assign // EXPERT_BLOCK) + N_EXPERTS
    n_rows = n_blocks * EXPERT_BLOCK
    row_tok = jnp.full((n_rows,), t, jnp.int32).at[dest].set((order // TOP_K).astype(jnp.int32))
    row_gate = jnp.zeros((n_rows,), F32).at[dest].set(gate.reshape(n_assign)[order])
    x_pad = jnp.concatenate([xt, jnp.zeros((1, D_MODEL), xt.dtype)], axis=0)
    x_rows = x_pad[row_tok].reshape(n_blocks, EXPERT_BLOCK, D_MODEL)
    block_e = jnp.minimum(
        jnp.searchsorted(padded_end, jnp.arange(n_blocks) * EXPERT_BLOCK, side='right'),
        N_EXPERTS - 1)

    def expert_block(args):
        xb, e = args
        h = xb @ w_gu[e] + b_gu[e]
        g = jnp.minimum(h[:, :D_EXPERT], SWIGLU_LIMIT)
        up = jnp.clip(h[:, D_EXPERT:], -SWIGLU_LIMIT, SWIGLU_LIMIT)
        act = (up + 1.0) * (g * jax.nn.sigmoid(SWIGLU_ALPHA * g))
        return act @ w_dn[e] + b_dn[e]

    y_rows = lax.map(expert_block, (x_rows, block_e)).reshape(n_rows, D_MODEL)
    y = jnp.zeros((t + 1, D_MODEL), F32).at[row_tok].add(y_rows.astype(F32) * row_gate[:, None])
    return y[:t].reshape(shape)


def _trunk(x, pos, h0, past, p):
    dtype = x.dtype
    cos, sin = rope_cos_sin(pos)
    new_re, new_im = [], []
    lat = None
    shared = None
    for layer in range(DEPTH):
        if layer < N_A_LAYERS:
            a = layer
            init = None if h0 is None else (h0[0][a], h0[1][a])
            mix, (hr, hi) = s5_mixer(x, p['ssm_lam_re'][a], p['ssm_lam_im'][a], p['ssm_log_dt'][a],
                                     p['ssm_b_re'][a], p['ssm_b_im'][a], p['ssm_c_re'][a],
                                     p['ssm_c_im'][a], p['ssm_d'][a], p['ssm_w_glu'][a],
                                     p['ssm_b_glu'][a], init)
            new_re.append(hr)
            new_im.append(hi)
        else:
            b = layer - N_A_LAYERS
            if lat is None:
                lat = mla_latent(x, p['mla_w_kv_a'], p['mla_g_kv'], cos, sin)
                if past is None:
                    k_nope = jnp.einsum('nlc,chd->nlhd', lat[0], p['mla_w_uk'])
                    v = jnp.einsum('nlc,chd->nlhd', lat[0], p['mla_w_uv'])
                    shared = (k_nope, lat[1], v)
                else:
                    c_all = jnp.concatenate([past[0].astype(dtype), lat[0]], axis=1)
                    kr_all = jnp.concatenate([past[1].astype(dtype), lat[1]], axis=1)
                    shared = (c_all, kr_all, jnp.arange(c_all.shape[1]))
            q_nope, q_pe = mla_queries(x, p['mla_w_q_a'][b], p['mla_g_q'][b], p['mla_w_q_b'][b], cos, sin)
            if past is None:
                o = causal_block_attention(q_nope, q_pe, shared[0], shared[1], shared[2])
            else:
                o = latent_attention(q_nope, q_pe, shared[0], shared[1], p['mla_w_uk'], p['mla_w_uv'],
                                     pos, shared[2])
            mix = o.astype(dtype) @ p['mla_w_o'][b]
        x = layer_norm(DEEPNORM_ALPHA * x + mix, p['ln_mix_g'][layer], p['ln_mix_b'][layer]).astype(dtype)
        f = moe_ffn(x, p['moe_w_router'][layer], p['moe_b_router'][layer], p['moe_w_gu'][layer],
                    p['moe_b_gu'][layer], p['moe_w_dn'][layer], p['moe_b_dn'][layer])
        x = layer_norm(DEEPNORM_ALPHA * x + f, p['ln_ffn_g'][layer], p['ln_ffn_b'][layer]).astype(dtype)
    return x, jnp.stack(new_re), jnp.stack(new_im), lat[0], lat[1]


def setup_inputs(seed: int = 0) -> dict:
    key = jax.random.key(seed)
    ks = iter(jax.random.split(key, 48))

    def nrm(shape, scale):
        return jax.random.normal(next(ks), shape, F32) * scale

    n_pages = PAST_LEN // PAGE_SIZE
    n_used = DEC_BATCH * n_pages
    n_pool = n_used + max(1, n_used // 4)
    hq = N_HEADS * (QK_NOPE + QK_ROPE)
    inp = {}
    inp['x_prompt'] = nrm((BATCH, SEQ, D_MODEL), 1.0)
    inp['x_sample'] = nrm((DEC_BATCH, DEC_SEQ, D_MODEL), 1.0)
    inp['state_ssm_re'] = nrm((N_A_LAYERS, DEC_BATCH, SSM_GROUPS, SSM_STATE), 0.1)
    inp['state_ssm_im'] = nrm((N_A_LAYERS, DEC_BATCH, SSM_GROUPS, SSM_STATE), 0.1)
    inp['cache_ckv'] = nrm((n_pool, PAGE_SIZE, KV_LORA), 1.0)
    inp['cache_krope'] = nrm((n_pool, PAGE_SIZE, QK_ROPE), 1.0)
    inp['page_table'] = jax.random.permutation(next(ks), n_pool)[:n_used].reshape(DEC_BATCH, n_pages).astype(jnp.int32)
    inp['ssm_lam_re'] = -0.5 + nrm((N_A_LAYERS, SSM_GROUPS, SSM_STATE), 0.01)
    inp['ssm_lam_im'] = jnp.pi * jnp.arange(SSM_STATE, dtype=F32) + nrm((N_A_LAYERS, SSM_GROUPS, SSM_STATE), 0.01)
    inp['ssm_log_dt'] = jax.random.uniform(next(ks), (N_A_LAYERS, SSM_GROUPS), F32,
                                           minval=math.log(DT_MIN), maxval=math.log(DT_MAX))
    inp['ssm_b_re'] = nrm((N_A_LAYERS, SSM_GROUPS, SSM_STATE, SSM_GROUP), (2 * SSM_GROUP) ** -0.5)
    inp['ssm_b_im'] = nrm((N_A_LAYERS, SSM_GROUPS, SSM_STATE, SSM_GROUP), (2 * SSM_GROUP) ** -0.5)
    inp['ssm_c_re'] = nrm((N_A_LAYERS, SSM_GROUPS, SSM_GROUP, SSM_STATE), (2 * SSM_STATE) ** -0.5)
    inp['ssm_c_im'] = nrm((N_A_LAYERS, SSM_GROUPS, SSM_GROUP, SSM_STATE), (2 * SSM_STATE) ** -0.5)
    inp['ssm_d'] = nrm((N_A_LAYERS, D_MODEL), 1.0)
    inp['ssm_w_glu'] = jnp.concatenate(
        [nrm((N_A_LAYERS, D_MODEL, D_MODEL), D_MODEL ** -0.5 * DEEPNORM_BETA),
         nrm((N_A_LAYERS, D_MODEL, D_MODEL), D_MODEL ** -0.5)], axis=-1)
    inp['ssm_b_glu'] = nrm((N_A_LAYERS, 2 * D_MODEL), 0.01)
    inp['mla_w_kv_a'] = nrm((D_MODEL, KV_LORA + QK_ROPE), D_MODEL ** -0.5)
    inp['mla_g_kv'] = 1.0 + nrm((KV_LORA,), 0.01)
    inp['mla_w_uk'] = nrm((KV_LORA, N_HEADS, QK_NOPE), KV_LORA ** -0.5)
    inp['mla_w_uv'] = nrm((KV_LORA, N_HEADS, V_HEAD), KV_LORA ** -0.5 * DEEPNORM_BETA)
    inp['mla_w_q_a'] = nrm((N_B_LAYERS, D_MODEL, Q_LORA), D_MODEL ** -0.5)
    inp['mla_g_q'] = 1.0 + nrm((N_B_LAYERS, Q_LORA), 0.01)
    inp['mla_w_q_b'] = nrm((N_B_LAYERS, Q_LORA, hq), Q_LORA ** -0.5)
    inp['mla_w_o'] = nrm((N_B_LAYERS, N_HEADS * V_HEAD, D_MODEL), (N_HEADS * V_HEAD) ** -0.5 * DEEPNORM_BETA)
    inp['moe_w_router'] = nrm((DEPTH, D_MODEL, N_EXPERTS), D_MODEL ** -0.5)
    inp['moe_b_router'] = nrm((DEPTH, N_EXPERTS), 0.01)
    inp['moe_w_gu'] = nrm((DEPTH, N_EXPERTS, D_MODEL, 2 * D_EXPERT), D_MODEL ** -0.5)
    inp['moe_b_gu'] = nrm((DEPTH, N_EXPERTS, 2 * D_EXPERT), 0.01)
    inp['moe_w_dn'] = nrm((DEPTH, N_EXPERTS, D_EXPERT, D_MODEL), D_EXPERT ** -0.5 * DEEPNORM_BETA)
    inp['moe_b_dn'] = nrm((DEPTH, N_EXPERTS, D_MODEL), 0.01)
    inp['ln_mix_g'] = 1.0 + nrm((DEPTH, D_MODEL), 0.01)
    inp['ln_mix_b'] = nrm((DEPTH, D_MODEL), 0.01)
    inp['ln_ffn_g'] = 1.0 + nrm((DEPTH, D_MODEL), 0.01)
    inp['ln_ffn_b'] = nrm((DEPTH, D_MODEL), 0.01)
    return inp


def reference(x_prompt, x_sample, state_ssm_re, state_ssm_im, cache_ckv, cache_krope, page_table,
              ssm_lam_re, ssm_lam_im, ssm_log_dt, ssm_b_re, ssm_b_im, ssm_c_re, ssm_c_im, ssm_d,
              ssm_w_glu, ssm_b_glu, mla_w_kv_a, mla_g_kv, mla_w_uk, mla_w_uv, mla_w_q_a, mla_g_q,
              mla_w_q_b, mla_w_o, moe_w_router, moe_b_router, moe_w_gu, moe_b_gu, moe_w_dn, moe_b_dn,
              ln_mix_g, ln_mix_b, ln_ffn_g, ln_ffn_b):
    p = dict(ssm_lam_re=ssm_lam_re, ssm_lam_im=ssm_lam_im, ssm_log_dt=ssm_log_dt,
             ssm_b_re=ssm_b_re, ssm_b_im=ssm_b_im, ssm_c_re=ssm_c_re, ssm_c_im=ssm_c_im,
             ssm_d=ssm_d, ssm_w_glu=ssm_w_glu, ssm_b_glu=ssm_b_glu,
             mla_w_kv_a=mla_w_kv_a, mla_g_kv=mla_g_kv, mla_w_uk=mla_w_uk, mla_w_uv=mla_w_uv,
             mla_w_q_a=mla_w_q_a, mla_g_q=mla_g_q, mla_w_q_b=mla_w_q_b, mla_w_o=mla_w_o,
             moe_w_router=moe_w_router, moe_b_router=moe_b_router, moe_w_gu=moe_w_gu,
             moe_b_gu=moe_b_gu, moe_w_dn=moe_w_dn, moe_b_dn=moe_b_dn,
             ln_mix_g=ln_mix_g, ln_mix_b=ln_mix_b, ln_ffn_g=ln_ffn_g, ln_ffn_b=ln_ffn_b)
    pos_prompt = jnp.arange(x_prompt.shape[1])
    y_prompt, ssm_re_prompt, ssm_im_prompt, ckv_prompt, krope_prompt = _trunk(
        x_prompt, pos_prompt, None, None, p)
    dec_b, n_pages = page_table.shape
    page = cache_ckv.shape[1]
    past_len = n_pages * page
    c_past = cache_ckv[page_table].reshape(dec_b, past_len, KV_LORA)
    kr_past = cache_krope[page_table].reshape(dec_b, past_len, QK_ROPE)
    pos_sample = past_len + jnp.arange(x_sample.shape[1])
    y_sample, ssm_re_sample, ssm_im_sample, ckv_sample, krope_sample = _trunk(
        x_sample, pos_sample, (state_ssm_re, state_ssm_im), (c_past, kr_past), p)
    return (y_prompt, y_sample, ssm_re_prompt, ssm_im_prompt, ckv_prompt, krope_prompt,
            ssm_re_sample, ssm_im_sample, ckv_sample, krope_sample)
```

```python
import functools
import math

import jax
import jax.numpy as jnp
from jax import lax
from jax.experimental import pallas as pl
from jax.experimental.pallas import tpu as pltpu

F32 = jnp.float32
BF16 = jnp.bfloat16

SSM_GROUP = 16
SSM_STATE = 64
QK_NOPE = 64
QK_ROPE = 32
V_HEAD = 64
ROPE_THETA = 10000.0
ATTN_SCALE = (QK_NOPE + QK_ROPE) ** -0.5
TOP_K = 4
SWIGLU_LIMIT = 7.0
SWIGLU_ALPHA = 1.702
LN_EPS = 1e-5
RMS_EPS = 1e-6
DEPTH = 4
DEEPNORM_ALPHA = (2 * DEPTH) ** 0.25

LANES = 128
SUBLANES = 8
HEAD_PAD = LANES
VMEM_LIMIT = 56 * 1024 * 1024

S5_CHUNK = LANES
S5_GROUPS_PER_CHUNK = S5_CHUNK // SSM_GROUP
S5_STATES_PER_CHUNK = S5_GROUPS_PER_CHUNK * SSM_STATE


def _params(sem, vmem=VMEM_LIMIT):
    return pltpu.CompilerParams(dimension_semantics=sem, vmem_limit_bytes=vmem)


def _layer_norm(x, g, b):
    mu = jnp.mean(x, axis=-1, keepdims=True)
    xc = x - mu
    var = jnp.mean(xc * xc, axis=-1, keepdims=True)
    return xc * lax.rsqrt(var + LN_EPS) * g + b


def _rms_norm(x, g):
    return x * lax.rsqrt(jnp.mean(x * x, axis=-1, keepdims=True) + RMS_EPS) * g


def _gelu_tanh(x):
    c = math.sqrt(2.0 / math.pi)
    return 0.5 * x * (1.0 + jnp.tanh(c * (x + 0.044715 * (x * x * x))))


def _s5_kernel(x_ref, bblk_ref, cblk_ref, are_ref, aim_ref, d_ref, h0r_ref, h0i_ref,
               z_ref, hr_out, hi_out, hbuf, st_re, st_im, *, n_seq, lt):
    i = pl.program_id(1)
    ns = S5_STATES_PER_CHUNK

    @pl.when(i == 0)
    def _():
        st_re[...] = h0r_ref[...]
        st_im[...] = h0i_ref[...]

    u = x_ref[...]
    hbuf[...] = jnp.dot(u.astype(BF16), bblk_ref[...], preferred_element_type=F32)
    a_re = jnp.broadcast_to(are_ref[...], (n_seq, ns))
    a_im = jnp.broadcast_to(aim_ref[...], (n_seq, ns))

    def step(l, carry):
        h_re, h_im = carry
        r0 = pl.multiple_of(l * n_seq, n_seq)
        bu_re = hbuf[pl.ds(r0, n_seq), 0:ns]
        bu_im = hbuf[pl.ds(r0, n_seq), ns:2 * ns]
        n_re = a_re * h_re - a_im * h_im + bu_re
        n_im = a_re * h_im + a_im * h_re + bu_im
        hbuf[pl.ds(r0, n_seq), 0:ns] = n_re
        hbuf[pl.ds(r0, n_seq), ns:2 * ns] = n_im
        return n_re, n_im

    h_re, h_im = lax.fori_loop(0, lt, step, (st_re[...], st_im[...]), unroll=min(lt, 8))
    st_re[...] = h_re
    st_im[...] = h_im

    y = jnp.dot(hbuf[...].astype(BF16), cblk_ref[...], preferred_element_type=F32)
    z_ref[...] = _gelu_tanh(y + d_ref[...] * u)

    @pl.when(i == pl.num_programs(1) - 1)
    def _():
        hr_out[...] = h_re
        hi_out[...] = h_im


def _s5_scan(x_tm, bblk, cblk, a_re, a_im, d_skip, h0_re, h0_im, *, n_seq, lt):
    rows, d_model = x_tm.shape
    n_chunks = d_model // S5_CHUNK
    ns = S5_STATES_PER_CHUNK
    seq = rows // n_seq
    tile = lt * n_seq
    kern = functools.partial(_s5_kernel, n_seq=n_seq, lt=lt)
    return pl.pallas_call(
        kern,
        grid=(n_chunks, seq // lt),
        in_specs=[
            pl.BlockSpec((tile, S5_CHUNK), lambda c, i: (i, c)),
            pl.BlockSpec((None, S5_CHUNK, 2 * ns), lambda c, i: (c, 0, 0)),
            pl.BlockSpec((None, 2 * ns, S5_CHUNK), lambda c, i: (c, 0, 0)),
            pl.BlockSpec((None, 1, ns), lambda c, i: (c, 0, 0)),
            pl.BlockSpec((None, 1, ns), lambda c, i: (c, 0, 0)),
            pl.BlockSpec((1, S5_CHUNK), lambda c, i: (0, c)),
            pl.BlockSpec((n_seq, ns), lambda c, i: (0, c)),
            pl.BlockSpec((n_seq, ns), lambda c, i: (0, c)),
        ],
        out_specs=[
            pl.BlockSpec((tile, S5_CHUNK), lambda c, i: (i, c)),
            pl.BlockSpec((n_seq, ns), lambda c, i: (0, c)),
            pl.BlockSpec((n_seq, ns), lambda c, i: (0, c)),
        ],
        out_shape=[
            jax.ShapeDtypeStruct((rows, d_model), F32),
            jax.ShapeDtypeStruct((n_seq, n_chunks * ns), F32),
            jax.ShapeDtypeStruct((n_seq, n_chunks * ns), F32),
        ],
        scratch_shapes=[
            pltpu.VMEM((tile, 2 * ns), F32),
            pltpu.VMEM((n_seq, ns), F32),
            pltpu.VMEM((n_seq, ns), F32),
        ],
        compiler_params=_params(("arbitrary", "arbitrary")),
        name="s5_scan",
    )(x_tm, bblk, cblk, a_re, a_im, d_skip, h0_re, h0_im)


def _proj_ln_router_kernel(a_ref, w_ref, b_ref, x_ref, g_ref, beta_ref, wr_ref, br_ref,
                           x1_ref, lg_ref, *, glu):
    h = jnp.dot(a_ref[...].astype(BF16), w_ref[...], preferred_element_type=F32) + b_ref[...]
    if glu:
        d = h.shape[1] // 2
        mix = h[:, :d] * jax.nn.sigmoid(h[:, d:])
    else:
        mix = h
    x1 = _layer_norm(DEEPNORM_ALPHA * x_ref[...] + mix, g_ref[...], beta_ref[...])
    x1_ref[...] = x1
    lg_ref[...] = jnp.dot(x1, wr_ref[...], preferred_element_type=F32,
                          precision=lax.Precision.HIGHEST) + br_ref[...]


def _proj_ln_router(a, w, b, x, g, beta, wr, br, *, glu, tm):
    t, k = a.shape
    nw = w.shape[1]
    d = x.shape[1]
    ne = wr.shape[1]
    kern = functools.partial(_proj_ln_router_kernel, glu=glu)
    return pl.pallas_call(
        kern,
        grid=(t // tm,),
        in_specs=[
            pl.BlockSpec((tm, k), lambda i: (i, 0)),
            pl.BlockSpec((k, nw), lambda i: (0, 0)),
            pl.BlockSpec((1, nw), lambda i: (0, 0)),
            pl.BlockSpec((tm, d), lambda i: (i, 0)),
            pl.BlockSpec((1, d), lambda i: (0, 0)),
            pl.BlockSpec((1, d), lambda i: (0, 0)),
            pl.BlockSpec((d, ne), lambda i: (0, 0)),
            pl.BlockSpec((1, ne), lambda i: (0, 0)),
        ],
        out_specs=[
            pl.BlockSpec((tm, d), lambda i: (i, 0)),
            pl.BlockSpec((tm, ne), lambda i: (i, 0)),
        ],
        out_shape=[
            jax.ShapeDtypeStruct((t, d), F32),
            jax.ShapeDtypeStruct((t, ne), F32),
        ],
        compiler_params=_params(("arbitrary",)),
        name="proj_ln_router",
    )(a, w, b, x, g, beta, wr, br)


def _route_kernel(lg_ref, tri_ref, eidx_ref, rank_ref, gate_ref, cnt_ref, base_scr):
    i = pl.program_id(0)

    @pl.when(i == 0)
    def _():
        base_scr[...] = jnp.zeros_like(base_scr)

    l = lg_ref[...]
    tt, ne = l.shape
    lane = lax.broadcasted_iota(jnp.int32, (tt, ne), 1).astype(F32)
    tops, idxs, hots = [], [], []
    for _ in range(TOP_K):
        m = jnp.max(l, axis=1, keepdims=True)
        idx = jnp.min(jnp.where(l == m, lane, float(ne)), axis=1, keepdims=True)
        hot = lane == idx
        tops.append(m)
        idxs.append(idx.astype(jnp.int32))
        hots.append(hot)
        l = jnp.where(hot, -jnp.inf, l)
    sel = jnp.zeros((tt, ne), F32)
    for hot in hots:
        sel = sel + jnp.where(hot, 1.0, 0.0)
    prefix = jnp.dot(tri_ref[...], sel.astype(BF16), preferred_element_type=F32) + base_scr[...]
    es = [jnp.exp(m - tops[0]) for m in tops]
    den = es[0] + es[1] + es[2] + es[3]
    wide = lax.broadcasted_iota(jnp.int32, (tt, LANES), 1)
    eidx_w = jnp.zeros((tt, LANES), jnp.int32)
    rank_w = jnp.zeros((tt, LANES), jnp.int32)
    gate_w = jnp.zeros((tt, LANES), F32)
    for k in range(TOP_K):
        rank_k = jnp.sum(jnp.where(hots[k], prefix, 0.0), axis=1, keepdims=True).astype(jnp.int32)
        eidx_w = jnp.where(wide == k, idxs[k], eidx_w)
        rank_w = jnp.where(wide == k, rank_k, rank_w)
        gate_w = jnp.where(wide == k, es[k] / den, gate_w)
    eidx_ref[...] = eidx_w
    rank_ref[...] = rank_w
    gate_ref[...] = gate_w
    base_scr[...] = base_scr[...] + jnp.sum(sel, axis=0, keepdims=True)
    cnt_ref[...] = base_scr[...].astype(jnp.int32)


def _route(logits, *, tt):
    t, ne = logits.shape
    tri = jnp.tri(tt, k=-1, dtype=BF16)
    return pl.pallas_call(
        _route_kernel,
        grid=(t // tt,),
        in_specs=[
            pl.BlockSpec((tt, ne), lambda i: (i, 0)),
            pl.BlockSpec((tt, tt), lambda i: (0, 0)),
        ],
        out_specs=[
            pl.BlockSpec((tt, LANES), lambda i: (i, 0)),
            pl.BlockSpec((tt, LANES), lambda i: (i, 0)),
            pl.BlockSpec((tt, LANES), lambda i: (i, 0)),
            pl.BlockSpec((1, ne), lambda i: (0, 0)),
        ],
        out_shape=[
            jax.ShapeDtypeStruct((t, LANES), jnp.int32),
            jax.ShapeDtypeStruct((t, LANES), jnp.int32),
            jax.ShapeDtypeStruct((t, LANES), F32),
            jax.ShapeDtypeStruct((1, ne), jnp.int32),
        ],
        scratch_shapes=[pltpu.VMEM((1, ne), F32)],
        compiler_params=_params(("arbitrary",)),
        name="moe_route",
    )(logits, tri)


def _scatter_kernel(dest_ref, x_ref, rows_in, rows_out, sem):
    del rows_in
    n_assign = dest_ref.shape[1]

    def row_copy(src_row, dst_row):
        return pltpu.make_async_copy(x_ref.at[pl.ds(src_row, 1)],
                                     rows_out.at[pl.ds(dst_row, 1)], sem)

    def issue(j, carry):
        row_copy(j // TOP_K, dest_ref[0, j]).start()
        return carry

    def drain(j, carry):
        row_copy(0, 0).wait()
        return carry

    lax.fori_loop(0, n_assign, issue, 0)
    lax.fori_loop(0, n_assign, drain, 0)


def _moe_scatter(x, dest, n_rows, *, tt):
    t, d = x.shape
    dest2 = dest.reshape(t // tt, 1, tt * TOP_K)
    rows0 = jnp.zeros((n_rows, d), F32)
    return pl.pallas_call(
        _scatter_kernel,
        grid=(t // tt,),
        in_specs=[
            pl.BlockSpec((None, 1, tt * TOP_K), lambda i: (i, 0, 0), memory_space=pltpu.SMEM),
            pl.BlockSpec((tt, d), lambda i: (i, 0)),
            pl.BlockSpec(memory_space=pl.ANY),
        ],
        out_specs=pl.BlockSpec(memory_space=pl.ANY),
        out_shape=jax.ShapeDtypeStruct((n_rows, d), F32),
        scratch_shapes=[pltpu.SemaphoreType.DMA(())],
        input_output_aliases={2: 0},
        compiler_params=_params(("arbitrary",)),
        name="moe_scatter",
    )(dest2, x, rows0)


def _ffn_kernel(be_ref, nu_ref, x_ref, wgu_ref, bgu_ref, wdn_ref, bdn_ref, y_ref, wgu_bf, wdn_bf):
    b = pl.program_id(0)
    prev = be_ref[jnp.maximum(b - 1, 0)]
    changed = jnp.logical_or(b == 0, be_ref[b] != prev)

    @pl.when(changed)
    def _():
        wgu_bf[...] = wgu_ref[...].astype(BF16)
        wdn_bf[...] = wdn_ref[...].astype(BF16)

    @pl.when(b < nu_ref[0])
    def _():
        de = wdn_bf.shape[0]
        h = jnp.dot(x_ref[...].astype(BF16), wgu_bf[...], preferred_element_type=F32) + bgu_ref[...]
        g = jnp.minimum(h[:, :de], SWIGLU_LIMIT)
        up = jnp.clip(h[:, de:], -SWIGLU_LIMIT, SWIGLU_LIMIT)
        act = (up + 1.0) * (g * jax.nn.sigmoid(SWIGLU_ALPHA * g))
        y_ref[...] = jnp.dot(act.astype(BF16), wdn_bf[...], preferred_element_type=F32) + bdn_ref[...]

    @pl.when(b >= nu_ref[0])
    def _():
        y_ref[...] = jnp.zeros_like(y_ref)


def _moe_ffn(x_rows, block_e, n_used, w_gu, b_gu, w_dn, b_dn, layer, *, bm):
    n_rows, d = x_rows.shape
    de = w_dn.shape[2]
    grid_spec = pltpu.PrefetchScalarGridSpec(
        num_scalar_prefetch=2,
        grid=(n_rows // bm,),
        in_specs=[
            pl.BlockSpec((bm, d), lambda b, be, nu: (b, 0)),
            pl.BlockSpec((None, None, d, 2 * de), lambda b, be, nu: (layer, be[b], 0, 0)),
            pl.BlockSpec((None, None, 1, 2 * de), lambda b, be, nu: (layer, be[b], 0, 0)),
            pl.BlockSpec((None, None, de, d), lambda b, be, nu: (layer, be[b], 0, 0)),
            pl.BlockSpec((None, None, 1, d), lambda b, be, nu: (layer, be[b], 0, 0)),
        ],
        out_specs=pl.BlockSpec((bm, d), lambda b, be, nu: (b, 0)),
        scratch_shapes=[pltpu.VMEM((d, 2 * de), BF16), pltpu.VMEM((de, d), BF16)],
    )
    return pl.pallas_call(
        _ffn_kernel,
        grid_spec=grid_spec,
        out_shape=jax.ShapeDtypeStruct((n_rows, d), F32),
        compiler_params=_params(("arbitrary",)),
        name="moe_ffn",
    )(block_e, n_used, x_rows, w_gu, b_gu, w_dn, b_dn)


def _combine_kernel(dcur_ref, dnxt_ref, gate_ref, x_ref, g_ref, beta_ref, yrows, out_ref, buf, sem):
    i = pl.program_id(0)
    n_steps = pl.num_programs(0)
    n_assign = dcur_ref.shape[1]

    def row_copy(dref, j, slot):
        return pltpu.make_async_copy(yrows.at[pl.ds(dref[0, j], 1)],
                                     buf.at[slot, j % TOP_K, pl.ds(j // TOP_K, 1)], sem.at[slot])

    def issue(dref, slot):
        def body(j, carry):
            row_copy(dref, j, slot).start()
            return carry
        lax.fori_loop(0, n_assign, body, 0)

    @pl.when(i == 0)
    def _():
        issue(dcur_ref, 0)

    @pl.when(i + 1 < n_steps)
    def _():
        issue(dnxt_ref, (i + 1) % 2)

    slot = i % 2

    def drain(j, carry):
        row_copy(dcur_ref, j, slot).wait()
        return carry

    lax.fori_loop(0, n_assign, drain, 0)

    gates = gate_ref[...]
    f = gates[:, 0:1] * buf[slot, 0]
    for k in range(1, TOP_K):
        f = f + gates[:, k:k + 1] * buf[slot, k]
    out_ref[...] = _layer_norm(DEEPNORM_ALPHA * x_ref[...] + f, g_ref[...], beta_ref[...])


def _moe_combine(y_rows, dest, gate_w, x1, g, beta, *, tt):
    t, d = x1.shape
    n_steps = t // tt
    dest2 = dest.reshape(n_steps, 1, tt * TOP_K)
    return pl.pallas_call(
        _combine_kernel,
        grid=(n_steps,),
        in_specs=[
            pl.BlockSpec((None, 1, tt * TOP_K), lambda i: (i, 0, 0), memory_space=pltpu.SMEM),
            pl.BlockSpec((None, 1, tt * TOP_K), lambda i: (jnp.minimum(i + 1, n_steps - 1), 0, 0),
                         memory_space=pltpu.SMEM),
            pl.BlockSpec((tt, LANES), lambda i: (i, 0)),
            pl.BlockSpec((tt, d), lambda i: (i, 0)),
            pl.BlockSpec((1, d), lambda i: (0, 0)),
            pl.BlockSpec((1, d), lambda i: (0, 0)),
            pl.BlockSpec(memory_space=pl.ANY),
        ],
        out_specs=pl.BlockSpec((tt, d), lambda i: (i, 0)),
        out_shape=jax.ShapeDtypeStruct((t, d), F32),
        scratch_shapes=[pltpu.VMEM((2, TOP_K, tt, d), F32), pltpu.SemaphoreType.DMA((2,))],
        compiler_params=_params(("arbitrary",)),
        name="moe_combine",
    )(dest2, dest2, gate_w, x1, g, beta, y_rows)


def _moe_layer(x1, logits, p, layer, *, bm, tt_route, tt_scatter, tt_combine):
    t, d = x1.shape
    ne = logits.shape[1]
    eidx_w, rank_w, gate_w, cnt = _route(logits, tt=tt_route)
    cnt = cnt[0]
    padded = (cnt + bm - 1) // bm * bm
    pend = jnp.cumsum(padded)
    pstart = pend - padded
    dest = pstart[eidx_w[:, :TOP_K]] + rank_w[:, :TOP_K]
    n_blocks = (t * TOP_K) // bm + ne
    n_used = pend[-1] // bm
    blk = jnp.arange(n_blocks, dtype=jnp.int32)
    be = jnp.minimum(jnp.searchsorted(pend, blk * bm, side="right"), ne - 1).astype(jnp.int32)
    be = jnp.where(blk < n_used, be, be[jnp.maximum(n_used - 1, 0)])
    x_rows = _moe_scatter(x1, dest.astype(jnp.int32), n_blocks * bm, tt=tt_scatter)
    y_rows = _moe_ffn(x_rows, be, n_used.reshape(1).astype(jnp.int32), p["moe_w_gu"], p["moe_b_gu4"],
                      p["moe_w_dn"], p["moe_b_dn4"], layer, bm=bm)
    return _moe_combine(y_rows, dest.astype(jnp.int32), gate_w, x1, p["ln_ffn_g"][layer][None],
                        p["ln_ffn_b"][layer][None], tt=tt_combine)


def _latent_kernel(x_ref, wc_ref, g_ref, wr_ref, wrs_ref, cos_ref, sin_ref, *rest, with_kv):
    if with_kv:
        wuk_ref, wuv_ref, c_ref, kr_ref, kp_ref, vp_ref = rest
    else:
        c_ref, kr_ref = rest
    xb = x_ref[...].astype(BF16)
    kv = jnp.dot(xb, wc_ref[...], preferred_element_type=F32)
    c = _rms_norm(kv, g_ref[...])
    r = jnp.dot(xb, wr_ref[...], preferred_element_type=F32)
    rs = jnp.dot(xb, wrs_ref[...], preferred_element_type=F32)
    kr = r * cos_ref[...] + rs * sin_ref[...]
    c_ref[...] = c
    kr_ref[...] = kr
    if with_kv:
        cb = c.astype(BF16)
        kn = jnp.dot(cb, wuk_ref[...], preferred_element_type=F32)
        n_heads = kn.shape[1] // HEAD_PAD
        for h in range(n_heads):
            sl = slice(h * HEAD_PAD, (h + 1) * HEAD_PAD)
            kp_ref[:, sl] = (kn[:, sl] + kr).astype(BF16)
        vp_ref[...] = jnp.dot(cb, wuv_ref[...], preferred_element_type=F32).astype(BF16)


def _mla_latent(x, wc, g_kv, wr, wrs, cos_k, sin_k, wuk_p, wuv_p, *, tm, with_kv):
    t, d = x.shape
    kvl = wc.shape[1]
    n_tab = cos_k.shape[0] // tm
    in_specs = [
        pl.BlockSpec((tm, d), lambda i: (i, 0)),
        pl.BlockSpec((d, kvl), lambda i: (0, 0)),
        pl.BlockSpec((1, kvl), lambda i: (0, 0)),
        pl.BlockSpec((d, HEAD_PAD), lambda i: (0, 0)),
        pl.BlockSpec((d, HEAD_PAD), lambda i: (0, 0)),
        pl.BlockSpec((tm, HEAD_PAD), lambda i: (i % n_tab, 0)),
        pl.BlockSpec((tm, HEAD_PAD), lambda i: (i % n_tab, 0)),
    ]
    out_specs = [
        pl.BlockSpec((tm, kvl), lambda i: (i, 0)),
        pl.BlockSpec((tm, HEAD_PAD), lambda i: (i, 0)),
    ]
    out_shape = [jax.ShapeDtypeStruct((t, kvl), F32), jax.ShapeDtypeStruct((t, HEAD_PAD), F32)]
    args = [x, wc, g_kv, wr, wrs, cos_k, sin_k]
    if with_kv:
        hp = wuk_p.shape[1]
        in_specs += [pl.BlockSpec((kvl, hp), lambda i: (0, 0)), pl.BlockSpec((kvl, hp), lambda i: (0, 0))]
        out_specs += [pl.BlockSpec((tm, hp), lambda i: (i, 0)), pl.BlockSpec((tm, hp), lambda i: (i, 0))]
        out_shape += [jax.ShapeDtypeStruct((t, hp), BF16), jax.ShapeDtypeStruct((t, hp), BF16)]
        args += [wuk_p, wuv_p]
    return pl.pallas_call(
        functools.partial(_latent_kernel, with_kv=with_kv),
        grid=(t // tm,),
        in_specs=in_specs,
        out_specs=out_specs,
        out_shape=out_shape,
        compiler_params=_params(("arbitrary",)),
        name="mla_latent",
    )(*args)


def _query_kernel(x_ref, wqa_ref, g_ref, wqb_ref, wqs_ref, cos_ref, sin_ref, q_ref):
    xb = x_ref[...].astype(BF16)
    cq = _rms_norm(jnp.dot(xb, wqa_ref[...], preferred_element_type=F32), g_ref[...]).astype(BF16)
    q = jnp.dot(cq, wqb_ref[...], preferred_element_type=F32)
    qs = jnp.dot(cq, wqs_ref[...], preferred_element_type=F32)
    cos = cos_ref[...]
    sin = sin_ref[...]
    n_heads = q.shape[1] // HEAD_PAD
    for h in range(n_heads):
        sl = slice(h * HEAD_PAD, (h + 1) * HEAD_PAD)
        q_ref[:, sl] = (q[:, sl] * cos + qs[:, sl] * sin).astype(BF16)


def _mla_queries(x, wqa, g_q, wqb_p, wqb_s, cos_q, sin_q, *, tm):
    t, d = x.shape
    ql = wqa.shape[1]
    hp = wqb_p.shape[1]
    n_tab = cos_q.shape[0] // tm
    return pl.pallas_call(
        _query_kernel,
        grid=(t // tm,),
        in_specs=[
            pl.BlockSpec((tm, d), lambda i: (i, 0)),
            pl.BlockSpec((d, ql), lambda i: (0, 0)),
            pl.BlockSpec((1, ql), lambda i: (0, 0)),
            pl.BlockSpec((ql, hp), lambda i: (0, 0)),
            pl.BlockSpec((ql, hp), lambda i: (0, 0)),
            pl.BlockSpec((tm, HEAD_PAD), lambda i: (i % n_tab, 0)),
            pl.BlockSpec((tm, HEAD_PAD), lambda i: (i % n_tab, 0)),
        ],
        out_specs=pl.BlockSpec((tm, hp), lambda i: (i, 0)),
        out_shape=jax.ShapeDtypeStruct((t, hp), BF16),
        compiler_params=_params(("arbitrary",)),
        name="mla_queries",
    )(x, wqa, g_q, wqb_p, wqb_s, cos_q, sin_q)


def _flash_kernel(q_ref, k_ref, v_ref, o_ref, *, tq):
    seq = q_ref.shape[0]
    n_tiles = seq // tq
    row = lax.broadcasted_iota(jnp.int32, (tq, tq), 0)
    col = lax.broadcasted_iota(jnp.int32, (tq, tq), 1)
    for i in range(n_tiles):
        q = q_ref[i * tq:(i + 1) * tq, :]
        m = jnp.full((tq, 1), -jnp.inf, F32)
        l = jnp.zeros((tq, 1), F32)
        acc = jnp.zeros((tq, HEAD_PAD), F32)
        for j in range(i + 1):
            k = k_ref[j * tq:(j + 1) * tq, :]
            v = v_ref[j * tq:(j + 1) * tq, :]
            s = lax.dot_general(q, k, (((1,), (1,)), ((), ())), preferred_element_type=F32) * ATTN_SCALE
            if j == i:
                s = jnp.where(col <= row, s, -jnp.inf)
            m_new = jnp.maximum(m, jnp.max(s, axis=1, keepdims=True))
            alpha = jnp.exp(m - m_new)
            pr = jnp.exp(s - m_new)
            l = alpha * l + jnp.sum(pr, axis=1, keepdims=True)
            acc = alpha * acc + jnp.dot(pr.astype(BF16), v, preferred_element_type=F32)
            m = m_new
        o_ref[i * tq:(i + 1) * tq, :] = (acc / l).astype(o_ref.dtype)


def _flash_attention(qp, kp, vp, *, n_seq, tq):
    t, hp = qp.shape
    seq = t // n_seq
    n_heads = hp // HEAD_PAD
    q3 = qp.reshape(n_seq, seq, hp)
    k3 = kp.reshape(n_seq, seq, hp)
    v3 = vp.reshape(n_seq, seq, hp)
    spec = pl.BlockSpec((None, seq, HEAD_PAD), lambda n, h: (n, 0, h))
    out = pl.pallas_call(
        functools.partial(_flash_kernel, tq=tq),
        grid=(n_seq, n_heads),
        in_specs=[spec, spec, spec],
        out_specs=spec,
        out_shape=jax.ShapeDtypeStruct((n_seq, seq, hp), BF16),
        compiler_params=_params(("arbitrary", "arbitrary")),
        name="flash_attention",
    )(q3, k3, v3)
    return out.reshape(t, hp)


def _qlat_kernel(q_ref, m_ref, o_ref):
    o_ref[...] = jnp.dot(q_ref[...], m_ref[...], preferred_element_type=F32).astype(o_ref.dtype)


def _q_latent(qp, m_heads):
    t, hp = qp.shape
    n_heads, _, width = m_heads.shape
    return pl.pallas_call(
        _qlat_kernel,
        grid=(n_heads,),
        in_specs=[
            pl.BlockSpec((t, HEAD_PAD), lambda h: (0, h)),
            pl.BlockSpec((None, HEAD_PAD, width), lambda h: (h, 0, 0)),
        ],
        out_specs=pl.BlockSpec((None, t, width), lambda h: (h, 0, 0)),
        out_shape=jax.ShapeDtypeStruct((n_heads, t, width), BF16),
        compiler_params=_params(("arbitrary",)),
        name="q_latent",
    )(qp, m_heads)


def _paged_kernel(pt_ref, q_ref, cnew_ref, rnew_ref, ckv_hbm, kr_hbm, o_ref,
                  cbuf, rbuf, sem_c, sem_r, m_scr, l_scr, acc_scr,
                  *, n_chunks, pages_per_step, n_pages, dec_seq, kvl):
    s = pl.program_id(0)
    n_steps = pl.num_programs(0)
    j = s % n_chunks
    page = cbuf.shape[2]

    def copies(step, slot, p):
        pg = pt_ref[(step // n_chunks) * n_pages + (step % n_chunks) * pages_per_step + p]
        return (pltpu.make_async_copy(ckv_hbm.at[pg], cbuf.at[slot, p], sem_c.at[slot]),
                pltpu.make_async_copy(kr_hbm.at[pg], rbuf.at[slot, p], sem_r.at[slot]))

    def issue(step, slot):
        for p in range(pages_per_step):
            cc, cr = copies(step, slot, p)
            cc.start()
            cr.start()

    @pl.when(s == 0)
    def _():
        issue(0, 0)

    @pl.when(s + 1 < n_steps)
    def _():
        issue(s + 1, (s + 1) % 2)

    slot = s % 2
    for p in range(pages_per_step):
        cc, cr = copies(s, slot, p)
        cc.wait()
        cr.wait()

    @pl.when(j == 0)
    def _():
        m_scr[...] = jnp.full_like(m_scr, -jnp.inf)
        l_scr[...] = jnp.zeros_like(l_scr)
        acc_scr[...] = jnp.zeros_like(acc_scr)

    q = q_ref[...]
    q_lat = q[:, :kvl]
    q_pe = q[:, kvl:kvl + QK_ROPE]
    nt = (((1,), (1,)), ((), ()))

    def update(c_b, r_b, mask):
        sc = (lax.dot_general(q_lat, c_b, nt, preferred_element_type=F32)
              + lax.dot_general(q_pe, r_b, nt, preferred_element_type=F32)) * ATTN_SCALE
        if mask is not None:
            sc = jnp.where(mask, sc, -jnp.inf)
        m_old = m_scr[...]
        m_new = jnp.maximum(m_old, jnp.max(sc, axis=1, keepdims=True))
        alpha = jnp.exp(m_old - m_new)
        pr = jnp.exp(sc - m_new)
        l_scr[...] = alpha * l_scr[...] + jnp.sum(pr, axis=1, keepdims=True)
        acc_scr[...] = alpha * acc_scr[...] + jnp.dot(pr.astype(BF16), c_b, preferred_element_type=F32)
        m_scr[...] = m_new

    c_b = cbuf[slot].reshape(pages_per_step * page, kvl).astype(BF16)
    r_b = rbuf[slot].reshape(pages_per_step * page, QK_ROPE).astype(BF16)
    update(c_b, r_b, None)

    @pl.when(j == n_chunks - 1)
    def _():
        rows = q.shape[0]
        n_new = cnew_ref.shape[0]
        q_l = lax.broadcasted_iota(jnp.int32, (rows, n_new), 0) % dec_seq
        kk = lax.broadcasted_iota(jnp.int32, (rows, n_new), 1)
        update(cnew_ref[...].astype(BF16), rnew_ref[...].astype(BF16), kk <= q_l)
        o_ref[...] = acc_scr[...] / l_scr[...]


def _paged_attention(qcat, c_new, kr_new, cache_ckv, cache_krope, page_table, *, dec_seq, n_heads,
                     pages_per_step):
    n_dec, n_pages = page_table.shape
    page, kvl = cache_ckv.shape[1:]
    rows = n_heads * dec_seq
    width = qcat.shape[1]
    n_chunks = n_pages // pages_per_step
    n_new = c_new.shape[1]
    kern = functools.partial(_paged_kernel, n_chunks=n_chunks, pages_per_step=pages_per_step,
                             n_pages=n_pages, dec_seq=dec_seq, kvl=kvl)
    grid_spec = pltpu.PrefetchScalarGridSpec(
        num_scalar_prefetch=1,
        grid=(n_dec * n_chunks,),
        in_specs=[
            pl.BlockSpec((rows, width), lambda s, pt: (s // n_chunks, 0)),
            pl.BlockSpec((None, n_new, kvl), lambda s, pt: (s // n_chunks, 0, 0)),
            pl.BlockSpec((None, n_new, QK_ROPE), lambda s, pt: (s // n_chunks, 0, 0)),
            pl.BlockSpec(memory_space=pl.ANY),
            pl.BlockSpec(memory_space=pl.ANY),
        ],
        out_specs=pl.BlockSpec((rows, kvl), lambda s, pt: (s // n_chunks, 0)),
        scratch_shapes=[
            pltpu.VMEM((2, pages_per_step, page, kvl), F32),
            pltpu.VMEM((2, pages_per_step, page, QK_ROPE), F32),
            pltpu.SemaphoreType.DMA((2,)),
            pltpu.SemaphoreType.DMA((2,)),
            pltpu.VMEM((rows, 1), F32),
            pltpu.VMEM((rows, 1), F32),
            pltpu.VMEM((rows, kvl), F32),
        ],
    )
    return pl.pallas_call(
        kern,
        grid_spec=grid_spec,
        out_shape=jax.ShapeDtypeStruct((n_dec * rows, kvl), F32),
        compiler_params=_params(("arbitrary",)),
        name="paged_attention",
    )(page_table.reshape(-1), qcat, c_new, kr_new, cache_ckv, cache_krope)


def _ov_kernel(o_ref, w_ref, out_ref):
    out_ref[...] = jnp.dot(o_ref[...].astype(BF16), w_ref[...], preferred_element_type=F32)


def _value_up(o_lat_hm, w_uv_hm):
    n_heads, t, kvl = o_lat_hm.shape
    vh = w_uv_hm.shape[2]
    return pl.pallas_call(
        _ov_kernel,
        grid=(n_heads,),
        in_specs=[
            pl.BlockSpec((None, t, kvl), lambda h: (h, 0, 0)),
            pl.BlockSpec((None, kvl, vh), lambda h: (h, 0, 0)),
        ],
        out_specs=pl.BlockSpec((None, t, vh), lambda h: (h, 0, 0)),
        out_shape=jax.ShapeDtypeStruct((n_heads, t, vh), F32),
        compiler_params=_params(("arbitrary",)),
        name="value_up",
    )(o_lat_hm, w_uv_hm)


def _s5_weights(lam_re, lam_im, log_dt, b_re, b_im, c_re, c_im):
    lr = lam_re.astype(F32)
    li = lam_im.astype(F32)
    dt = jnp.exp(log_dt.astype(F32))[:, None]
    mag = jnp.exp(lr * dt)
    ang = li * dt
    ab_re = mag * jnp.cos(ang)
    ab_im = mag * jnp.sin(ang)
    den = lr * lr + li * li
    f_re = ((ab_re - 1.0) * lr + ab_im * li) / den
    f_im = (ab_im * lr - (ab_re - 1.0) * li) / den
    br = b_re.astype(F32)
    bi = b_im.astype(F32)
    bb_re = f_re[..., None] * br - f_im[..., None] * bi
    bb_im = f_re[..., None] * bi + f_im[..., None] * br
    g = lr.shape[0]
    gpc = S5_GROUPS_PER_CHUNK
    n_chunks = g // gpc
    eye = jnp.eye(gpc, dtype=F32)

    def b_block(bb):
        t = bb.reshape(n_chunks, gpc, SSM_STATE, SSM_GROUP).transpose(0, 1, 3, 2)
        blk = t[:, :, :, None, :] * eye[None, :, None, :, None]
        return blk.reshape(n_chunks, gpc * SSM_GROUP, gpc * SSM_STATE)

    def c_block(cc):
        t = cc.reshape(n_chunks, gpc, SSM_GROUP, SSM_STATE).transpose(0, 1, 3, 2)
        blk = t[:, :, :, None, :] * eye[None, :, None, :, None]
        return blk.reshape(n_chunks, gpc * SSM_STATE, gpc * SSM_GROUP)

    bblk = jnp.concatenate([b_block(bb_re), b_block(bb_im)], axis=2).astype(BF16)
    cblk = jnp.concatenate([c_block(c_re.astype(F32)), -c_block(c_im.astype(F32))], axis=1).astype(BF16)
    a_re = ab_re.reshape(n_chunks, 1, gpc * SSM_STATE)
    a_im = ab_im.reshape(n_chunks, 1, gpc * SSM_STATE)
    return bblk, cblk, a_re, a_im


def _rope_tables(pos, n_rep):
    inv = 1.0 / (ROPE_THETA ** (jnp.arange(0, QK_ROPE, 2, dtype=F32) / QK_ROPE))
    ang = pos.astype(F32)[:, None] * inv[None, :]
    cos = jnp.cos(ang)
    sin = jnp.sin(ang)
    length = pos.shape[0]
    zero_lo = jnp.zeros((length, QK_NOPE), F32)
    zero_hi = jnp.zeros((length, HEAD_PAD - QK_NOPE - QK_ROPE), F32)
    cos_k = jnp.concatenate([zero_lo, cos, cos, zero_hi], axis=1)
    sin_k = jnp.concatenate([zero_lo, sin, sin, zero_hi], axis=1)
    cos_q = jnp.concatenate([jnp.ones((length, QK_NOPE), F32), cos, cos, zero_hi], axis=1)
    if n_rep > 1:
        cos_k, sin_k, cos_q = (jnp.tile(a, (n_rep, 1)) for a in (cos_k, sin_k, cos_q))
    return cos_k, sin_k, cos_q


def _pad_heads(w, n_heads, width, offset=0):
    k = w.shape[0]
    w3 = w.reshape(k, n_heads, width)
    out = jnp.zeros((k, n_heads, HEAD_PAD), w.dtype).at[:, :, offset:offset + width].set(w3)
    return out.reshape(k, n_heads * HEAD_PAD)


def _swap_rope(w_rope):
    half = QK_ROPE // 2
    return jnp.concatenate([-w_rope[..., half:], w_rope[..., :half]], axis=-1)


def _mla_weights(p, b):
    n_heads = p["mla_w_uk"].shape[1]
    kvl = p["mla_w_uk"].shape[0]
    w_qb = p["mla_w_q_b"][b]
    ql = w_qb.shape[0]
    w3 = w_qb.reshape(ql, n_heads, QK_NOPE + QK_ROPE)
    zero_tail = jnp.zeros((ql, n_heads, HEAD_PAD - QK_NOPE - QK_ROPE), F32)
    wqb_p = jnp.concatenate([w3, zero_tail], axis=2).reshape(ql, n_heads * HEAD_PAD)
    wqb_s = jnp.concatenate([jnp.zeros((ql, n_heads, QK_NOPE), F32), _swap_rope(w3[..., QK_NOPE:]),
                             zero_tail], axis=2).reshape(ql, n_heads * HEAD_PAD)
    w_o = p["mla_w_o"][b]
    d_model = w_o.shape[1]
    w_o_p = jnp.zeros((n_heads, HEAD_PAD, d_model), F32).at[:, :V_HEAD].set(
        w_o.reshape(n_heads, V_HEAD, d_model)).reshape(n_heads * HEAD_PAD, d_model)
    del kvl
    return dict(wqa=p["mla_w_q_a"][b].astype(BF16), g_q=p["mla_g_q"][b][None],
                wqb_p=wqb_p.astype(BF16), wqb_s=wqb_s.astype(BF16),
                w_o=w_o.astype(BF16), w_o_p=w_o_p.astype(BF16))


def _shared_mla_weights(p):
    w_kv_a = p["mla_w_kv_a"]
    d_model = w_kv_a.shape[0]
    kvl, n_heads, _ = p["mla_w_uk"].shape
    w_rope = w_kv_a[:, kvl:]
    pad_lo = jnp.zeros((d_model, QK_NOPE), F32)
    pad_hi = jnp.zeros((d_model, HEAD_PAD - QK_NOPE - QK_ROPE), F32)
    wr = jnp.concatenate([pad_lo, w_rope, pad_hi], axis=1)
    wrs = jnp.concatenate([pad_lo, _swap_rope(w_rope), pad_hi], axis=1)
    wuk_p = _pad_heads(p["mla_w_uk"].reshape(kvl, n_heads * QK_NOPE), n_heads, QK_NOPE)
    wuv_p = _pad_heads(p["mla_w_uv"].reshape(kvl, n_heads * V_HEAD), n_heads, V_HEAD)
    width = kvl + LANES
    m_heads = jnp.zeros((n_heads, HEAD_PAD, width), F32)
    m_heads = m_heads.at[:, :QK_NOPE, :kvl].set(p["mla_w_uk"].transpose(1, 2, 0))
    m_heads = m_heads.at[:, QK_NOPE:QK_NOPE + QK_ROPE, kvl:kvl + QK_ROPE].set(
        jnp.broadcast_to(jnp.eye(QK_ROPE, dtype=F32), (n_heads, QK_ROPE, QK_ROPE)))
    return dict(wc=w_kv_a[:, :kvl].astype(BF16), g_kv=p["mla_g_kv"][None], wr=wr.astype(BF16),
                wrs=wrs.astype(BF16), wuk_p=wuk_p.astype(BF16), wuv_p=wuv_p.astype(BF16),
                m_heads=m_heads.astype(BF16), w_uv_hm=p["mla_w_uv"].transpose(1, 0, 2).astype(BF16))


def _tiles(t):
    big = t >= 4096
    return dict(tm=256 if big else min(t, 128), bm=256 if big else 128,
                tt_route=512 if big else min(t, 128), tt_scatter=256 if big else min(t, 128),
                tt_combine=128 if big else min(t, 128))


def _trunk(x, pos, h0, past, p, s5w, mla_shared, mla_layers):
    n_seq, seq, d_model = x.shape
    t = n_seq * seq
    cfg = _tiles(t)
    n_a = len(s5w)
    n_states = d_model // SSM_GROUP * SSM_STATE

    def moe(x1, logits, layer):
        return _moe_layer(x1, logits, p, layer, bm=cfg["bm"], tt_route=cfg["tt_route"],
                          tt_scatter=cfg["tt_scatter"], tt_combine=cfg["tt_combine"])

    xt = x.transpose(1, 0, 2).reshape(t, d_model)
    lt = min(seq, 256)
    new_re, new_im = [], []
    for a in range(n_a):
        bblk, cblk, a_re, a_im = s5w[a]
        if h0 is None:
            h0r = jnp.zeros((n_seq, n_states), F32)
            h0i = jnp.zeros((n_seq, n_states), F32)
        else:
            h0r = h0[0][a].reshape(n_seq, n_states)
            h0i = h0[1][a].reshape(n_seq, n_states)
        z, hr, hi = _s5_scan(xt, bblk, cblk, a_re, a_im, p["ssm_d"][a][None], h0r, h0i,
                             n_seq=n_seq, lt=lt)
        new_re.append(hr.reshape(n_seq, d_model // SSM_GROUP, SSM_STATE))
        new_im.append(hi.reshape(n_seq, d_model // SSM_GROUP, SSM_STATE))
        x1, logits = _proj_ln_router(z, p["ssm_w_glu_bf"][a], p["ssm_b_glu"][a][None], xt,
                                     p["ln_mix_g"][a][None], p["ln_mix_b"][a][None],
                                     p["moe_w_router"][a], p["moe_b_router"][a][None],
                                     glu=True, tm=cfg["tm"])
        xt = moe(x1, logits, a)

    xb = xt.reshape(seq, n_seq, d_model).transpose(1, 0, 2).reshape(t, d_model)
    tm = cfg["tm"]
    n_rep = 1 if seq >= tm else tm // seq
    cos_k, sin_k, cos_q = _rope_tables(pos, n_rep)
    prompt = past is None
    lat = _mla_latent(xb, mla_shared["wc"], mla_shared["g_kv"], mla_shared["wr"], mla_shared["wrs"],
                      cos_k, sin_k, mla_shared["wuk_p"], mla_shared["wuv_p"], tm=tm, with_kv=prompt)
    c_lat, kr128 = lat[0], lat[1]
    kr = kr128[:, QK_NOPE:QK_NOPE + QK_ROPE]
    kvl = c_lat.shape[1]
    n_heads = mla_shared["m_heads"].shape[0]
    zero_bias = jnp.zeros((1, d_model), F32)
    if not prompt:
        cache_ckv, cache_krope, page_table = past
        n_new = -(-seq // SUBLANES) * SUBLANES
        c_new = jnp.zeros((n_seq, n_new, kvl), F32).at[:, :seq].set(c_lat.reshape(n_seq, seq, kvl))
        kr_new = jnp.zeros((n_seq, n_new, QK_ROPE), F32).at[:, :seq].set(kr.reshape(n_seq, seq, QK_ROPE))
    for b, mw in enumerate(mla_layers):
        layer = n_a + b
        qp = _mla_queries(xb, mw["wqa"], mw["g_q"], mw["wqb_p"], mw["wqb_s"], cos_q, sin_k, tm=tm)
        if prompt:
            o = _flash_attention(qp, lat[2], lat[3], n_seq=n_seq, tq=min(seq, 512))
            w_o = mw["w_o_p"]
        else:
            qcat = _q_latent(qp, mla_shared["m_heads"])
            width = qcat.shape[2]
            qcat = qcat.reshape(n_heads, n_seq, seq, width).transpose(1, 0, 2, 3).reshape(
                n_seq * n_heads * seq, width)
            o_lat = _paged_attention(qcat, c_new, kr_new, cache_ckv, cache_krope, page_table,
                                     dec_seq=seq, n_heads=n_heads, pages_per_step=16)
            o_hm = o_lat.reshape(n_seq, n_heads, seq, kvl).transpose(1, 0, 2, 3).reshape(n_heads, t, kvl)
            o = _value_up(o_hm, mla_shared["w_uv_hm"])
            o = o.transpose(1, 0, 2).reshape(t, n_heads * V_HEAD)
            w_o = mw["w_o"]
        x1, logits = _proj_ln_router(o, w_o, zero_bias, xb, p["ln_mix_g"][layer][None],
                                     p["ln_mix_b"][layer][None], p["moe_w_router"][layer],
                                     p["moe_b_router"][layer][None], glu=False, tm=tm)
        xb = moe(x1, logits, layer)
    y = xb.reshape(n_seq, seq, d_model)
    return (y, jnp.stack(new_re), jnp.stack(new_im), c_lat.reshape(n_seq, seq, kvl),
            kr.reshape(n_seq, seq, QK_ROPE))


def kernel(x_prompt, x_sample, state_ssm_re, state_ssm_im, cache_ckv, cache_krope, page_table,
           ssm_lam_re, ssm_lam_im, ssm_log_dt, ssm_b_re, ssm_b_im, ssm_c_re, ssm_c_im, ssm_d,
           ssm_w_glu, ssm_b_glu, mla_w_kv_a, mla_g_kv, mla_w_uk, mla_w_uv, mla_w_q_a, mla_g_q,
           mla_w_q_b, mla_w_o, moe_w_router, moe_b_router, moe_w_gu, moe_b_gu, moe_w_dn, moe_b_dn,
           ln_mix_g, ln_mix_b, ln_ffn_g, ln_ffn_b):
    p = dict(ssm_d=ssm_d, ssm_b_glu=ssm_b_glu, ssm_w_glu_bf=ssm_w_glu.astype(BF16),
             mla_w_kv_a=mla_w_kv_a, mla_g_kv=mla_g_kv, mla_w_uk=mla_w_uk, mla_w_uv=mla_w_uv,
             mla_w_q_a=mla_w_q_a, mla_g_q=mla_g_q, mla_w_q_b=mla_w_q_b, mla_w_o=mla_w_o,
             moe_w_router=moe_w_router, moe_b_router=moe_b_router, moe_w_gu=moe_w_gu,
             moe_b_gu4=moe_b_gu[:, :, None, :], moe_w_dn=moe_w_dn, moe_b_dn4=moe_b_dn[:, :, None, :],
             ln_mix_g=ln_mix_g, ln_mix_b=ln_mix_b, ln_ffn_g=ln_ffn_g, ln_ffn_b=ln_ffn_b)
    n_a = ssm_lam_re.shape[0]
    s5w = [_s5_weights(ssm_lam_re[a], ssm_lam_im[a], ssm_log_dt[a], ssm_b_re[a], ssm_b_im[a],
                       ssm_c_re[a], ssm_c_im[a]) for a in range(n_a)]
    mla_shared = _shared_mla_weights(p)
    mla_layers = [_mla_weights(p, b) for b in range(mla_w_q_a.shape[0])]

    pos_prompt = jnp.arange(x_prompt.shape[1])
    out_p = _trunk(x_prompt, pos_prompt, None, None, p, s5w, mla_shared, mla_layers)
    n_pages = page_table.shape[1]
    past_len = n_pages * cache_ckv.shape[1]
    pos_sample = past_len + jnp.arange(x_sample.shape[1])
    out_s = _trunk(x_sample, pos_sample, (state_ssm_re, state_ssm_im),
                   (cache_ckv, cache_krope, page_table), p, s5w, mla_shared, mla_layers)
    return (out_p[0], out_s[0], out_p[1], out_p[2], out_p[3], out_p[4],
            out_s[1], out_s[2], out_s[3], out_s[4])
```

```python
import functools
import math

import jax
import jax.numpy as jnp
from jax import lax
from jax.experimental import pallas as pl
from jax.experimental.pallas import tpu as pltpu

F32 = jnp.float32
BF16 = jnp.bfloat16

SSM_GROUP = 16
SSM_STATE = 64
QK_NOPE = 64
QK_ROPE = 32
V_HEAD = 64
ROPE_THETA = 10000.0
ATTN_SCALE = (QK_NOPE + QK_ROPE) ** -0.5
TOP_K = 4
SWIGLU_LIMIT = 7.0
SWIGLU_ALPHA = 1.702
LN_EPS = 1e-5
RMS_EPS = 1e-6
DEPTH = 4
DEEPNORM_ALPHA = (2 * DEPTH) ** 0.25

LANES = 128
SUBLANES = 8
ROW_TILE = SUBLANES
HEAD_PAD = LANES
DMA_UNROLL = 8
VMEM_LIMIT = 56 * 1024 * 1024

S5_CHUNK = LANES
S5_GROUPS_PER_CHUNK = S5_CHUNK // SSM_GROUP
S5_STATES_PER_CHUNK = S5_GROUPS_PER_CHUNK * SSM_STATE


def _params(sem, vmem=VMEM_LIMIT):
    return pltpu.CompilerParams(dimension_semantics=sem, vmem_limit_bytes=vmem)


def _layer_norm(x, g, b):
    mu = jnp.mean(x, axis=-1, keepdims=True)
    xc = x - mu
    var = jnp.mean(xc * xc, axis=-1, keepdims=True)
    return xc * lax.rsqrt(var + LN_EPS) * g + b


def _rms_norm(x, g):
    return x * lax.rsqrt(jnp.mean(x * x, axis=-1, keepdims=True) + RMS_EPS) * g


def _gelu_tanh(x):
    c = math.sqrt(2.0 / math.pi)
    return 0.5 * x * (1.0 + jnp.tanh(c * (x + 0.044715 * (x * x * x))))


def _s5_kernel(x_ref, bblk_ref, cblk_ref, are_ref, aim_ref, d_ref, h0r_ref, h0i_ref,
               z_ref, hr_out, hi_out, hbuf, st_re, st_im, *, n_seq, lt):
    i = pl.program_id(1)
    ns = S5_STATES_PER_CHUNK

    @pl.when(i == 0)
    def _():
        st_re[...] = h0r_ref[...]
        st_im[...] = h0i_ref[...]

    u = x_ref[...]
    hbuf[...] = jnp.dot(u.astype(BF16), bblk_ref[...], preferred_element_type=F32)
    a_re = jnp.broadcast_to(are_ref[...], (n_seq, ns))
    a_im = jnp.broadcast_to(aim_ref[...], (n_seq, ns))

    def step(l, carry):
        h_re, h_im = carry
        r0 = pl.multiple_of(l * n_seq, n_seq)
        bu_re = hbuf[pl.ds(r0, n_seq), 0:ns]
        bu_im = hbuf[pl.ds(r0, n_seq), ns:2 * ns]
        n_re = a_re * h_re - a_im * h_im + bu_re
        n_im = a_re * h_im + a_im * h_re + bu_im
        hbuf[pl.ds(r0, n_seq), 0:ns] = n_re
        hbuf[pl.ds(r0, n_seq), ns:2 * ns] = n_im
        return n_re, n_im

    h_re, h_im = lax.fori_loop(0, lt, step, (st_re[...], st_im[...]), unroll=min(lt, 8))
    st_re[...] = h_re
    st_im[...] = h_im

    y = jnp.dot(hbuf[...].astype(BF16), cblk_ref[...], preferred_element_type=F32)
    z_ref[...] = _gelu_tanh(y + d_ref[...] * u)

    @pl.when(i == pl.num_programs(1) - 1)
    def _():
        hr_out[...] = h_re
        hi_out[...] = h_im


def _s5_scan(x_tm, bblk, cblk, a_re, a_im, d_skip, h0_re, h0_im, *, n_seq, lt):
    rows, d_model = x_tm.shape
    n_chunks = d_model // S5_CHUNK
    ns = S5_STATES_PER_CHUNK
    seq = rows // n_seq
    tile = lt * n_seq
    kern = functools.partial(_s5_kernel, n_seq=n_seq, lt=lt)
    return pl.pallas_call(
        kern,
        grid=(n_chunks, seq // lt),
        in_specs=[
            pl.BlockSpec((tile, S5_CHUNK), lambda c, i: (i, c)),
            pl.BlockSpec((None, S5_CHUNK, 2 * ns), lambda c, i: (c, 0, 0)),
            pl.BlockSpec((None, 2 * ns, S5_CHUNK), lambda c, i: (c, 0, 0)),
            pl.BlockSpec((None, 1, ns), lambda c, i: (c, 0, 0)),
            pl.BlockSpec((None, 1, ns), lambda c, i: (c, 0, 0)),
            pl.BlockSpec((1, S5_CHUNK), lambda c, i: (0, c)),
            pl.BlockSpec((n_seq, ns), lambda c, i: (0, c)),
            pl.BlockSpec((n_seq, ns), lambda c, i: (0, c)),
        ],
        out_specs=[
            pl.BlockSpec((tile, S5_CHUNK), lambda c, i: (i, c)),
            pl.BlockSpec((n_seq, ns), lambda c, i: (0, c)),
            pl.BlockSpec((n_seq, ns), lambda c, i: (0, c)),
        ],
        out_shape=[
            jax.ShapeDtypeStruct((rows, d_model), F32),
            jax.ShapeDtypeStruct((n_seq, n_chunks * ns), F32),
            jax.ShapeDtypeStruct((n_seq, n_chunks * ns), F32),
        ],
        scratch_shapes=[
            pltpu.VMEM((tile, 2 * ns), F32),
            pltpu.VMEM((n_seq, ns), F32),
            pltpu.VMEM((n_seq, ns), F32),
        ],
        compiler_params=_params(("arbitrary", "arbitrary")),
        name="s5_scan",
    )(x_tm, bblk, cblk, a_re, a_im, d_skip, h0_re, h0_im)


def _split_bf16(v):
    hi = v.astype(BF16)
    return hi, (v - hi.astype(F32)).astype(BF16)


def _proj_ln_router_kernel(a_ref, w_ref, b_ref, x_ref, g_ref, beta_ref, wrh_ref, wrl_ref, br_ref,
                           x1_ref, xrow_ref, lg_ref, *, glu):
    h = jnp.dot(a_ref[...].astype(BF16), w_ref[...], preferred_element_type=F32) + b_ref[...]
    if glu:
        d = h.shape[1] // 2
        mix = h[:, :d] * jax.nn.sigmoid(h[:, d:])
    else:
        mix = h
    x1 = _layer_norm(DEEPNORM_ALPHA * x_ref[...] + mix, g_ref[...], beta_ref[...])
    x1_ref[...] = x1
    tm, d_model = x1.shape
    for c in range(d_model // LANES):
        xrow_ref[pl.ds(c, tm, stride=ROW_TILE), :] = x1[:, c * LANES:(c + 1) * LANES]
    xh, xl = _split_bf16(x1)
    lg_ref[...] = (jnp.dot(xh, wrh_ref[...], preferred_element_type=F32)
                   + jnp.dot(xl, wrh_ref[...], preferred_element_type=F32)
                   + jnp.dot(xh, wrl_ref[...], preferred_element_type=F32)) + br_ref[...]


def _proj_ln_router(a, w, b, x, g, beta, wr, br, *, glu, tm):
    t, k = a.shape
    nw = w.shape[1]
    d = x.shape[1]
    ne = wr.shape[1]
    wrh, wrl = _split_bf16(wr.astype(F32))
    kern = functools.partial(_proj_ln_router_kernel, glu=glu)
    return pl.pallas_call(
        kern,
        grid=(t // tm,),
        in_specs=[
            pl.BlockSpec((tm, k), lambda i: (i, 0)),
            pl.BlockSpec((k, nw), lambda i: (0, 0)),
            pl.BlockSpec((1, nw), lambda i: (0, 0)),
            pl.BlockSpec((tm, d), lambda i: (i, 0)),
            pl.BlockSpec((1, d), lambda i: (0, 0)),
            pl.BlockSpec((1, d), lambda i: (0, 0)),
            pl.BlockSpec((d, ne), lambda i: (0, 0)),
            pl.BlockSpec((d, ne), lambda i: (0, 0)),
            pl.BlockSpec((1, ne), lambda i: (0, 0)),
        ],
        out_specs=[
            pl.BlockSpec((tm, d), lambda i: (i, 0)),
            pl.BlockSpec((tm * ROW_TILE, LANES), lambda i: (i, 0)),
            pl.BlockSpec((tm, ne), lambda i: (i, 0)),
        ],
        out_shape=[
            jax.ShapeDtypeStruct((t, d), F32),
            jax.ShapeDtypeStruct((t * ROW_TILE, LANES), F32),
            jax.ShapeDtypeStruct((t, ne), F32),
        ],
        compiler_params=_params(("arbitrary",)),
        name="proj_ln_router",
    )(a, w, b, x, g, beta, wrh, wrl, br)


def _route_kernel(lg_ref, tri_ref, eidx_ref, rank_ref, gate_ref, cnt_ref, base_scr):
    i = pl.program_id(0)

    @pl.when(i == 0)
    def _():
        base_scr[...] = jnp.zeros_like(base_scr)

    l = lg_ref[...]
    tt, ne = l.shape
    lane = lax.broadcasted_iota(jnp.int32, (tt, ne), 1).astype(F32)
    tops, idxs, hots = [], [], []
    for _ in range(TOP_K):
        m = jnp.max(l, axis=1, keepdims=True)
        idx = jnp.min(jnp.where(l == m, lane, float(ne)), axis=1, keepdims=True)
        hot = lane == idx
        tops.append(m)
        idxs.append(idx.astype(jnp.int32))
        hots.append(hot)
        l = jnp.where(hot, -jnp.inf, l)
    sel = jnp.zeros((tt, ne), F32)
    for hot in hots:
        sel = sel + jnp.where(hot, 1.0, 0.0)
    prefix = jnp.dot(tri_ref[...], sel.astype(BF16), preferred_element_type=F32) + base_scr[...]
    es = [jnp.exp(m - tops[0]) for m in tops]
    den = es[0] + es[1] + es[2] + es[3]
    wide = lax.broadcasted_iota(jnp.int32, (tt, LANES), 1)
    eidx_w = jnp.zeros((tt, LANES), jnp.int32)
    rank_w = jnp.zeros((tt, LANES), jnp.int32)
    gate_w = jnp.zeros((tt, LANES), F32)
    for k in range(TOP_K):
        rank_k = jnp.sum(jnp.where(hots[k], prefix, 0.0), axis=1, keepdims=True).astype(jnp.int32)
        eidx_w = jnp.where(wide == k, idxs[k], eidx_w)
        rank_w = jnp.where(wide == k, rank_k, rank_w)
        gate_w = jnp.where(wide == k, es[k] / den, gate_w)
    eidx_ref[...] = eidx_w
    rank_ref[...] = rank_w
    gate_ref[...] = gate_w
    base_scr[...] = base_scr[...] + jnp.sum(sel, axis=0, keepdims=True)
    cnt_ref[...] = base_scr[...].astype(jnp.int32)


def _route(logits, *, tt):
    t, ne = logits.shape
    tri = jnp.tri(tt, k=-1, dtype=BF16)
    return pl.pallas_call(
        _route_kernel,
        grid=(t // tt,),
        in_specs=[
            pl.BlockSpec((tt, ne), lambda i: (i, 0)),
            pl.BlockSpec((tt, tt), lambda i: (0, 0)),
        ],
        out_specs=[
            pl.BlockSpec((tt, LANES), lambda i: (i, 0)),
            pl.BlockSpec((tt, LANES), lambda i: (i, 0)),
            pl.BlockSpec((tt, LANES), lambda i: (i, 0)),
            pl.BlockSpec((1, ne), lambda i: (0, 0)),
        ],
        out_shape=[
            jax.ShapeDtypeStruct((t, LANES), jnp.int32),
            jax.ShapeDtypeStruct((t, LANES), jnp.int32),
            jax.ShapeDtypeStruct((t, LANES), F32),
            jax.ShapeDtypeStruct((1, ne), jnp.int32),
        ],
        scratch_shapes=[pltpu.VMEM((1, ne), F32)],
        compiler_params=_params(("arbitrary",)),
        name="moe_route",
    )(logits, tri)


def _tile_rows(row):
    return pl.ds(pl.multiple_of(row * ROW_TILE, ROW_TILE), ROW_TILE)


def _scatter_kernel(dest_ref, pad_ref, x_ref, rows_out, zero_row, sem, zsem):
    i = pl.program_id(0)
    n_assign = dest_ref.shape[1]
    tt = n_assign // TOP_K

    def row_copy(src_row, dst_row):
        return pltpu.make_async_copy(x_ref.at[_tile_rows(src_row)], rows_out.at[_tile_rows(dst_row)], sem)

    def drain(j, carry):
        row_copy(0, 0).wait()
        return carry

    for k in range(TOP_K):
        def issue(t, carry, k=k):
            row_copy(t, dest_ref[0, k * tt + t]).start()
            return carry
        lax.fori_loop(0, tt, issue, 0, unroll=DMA_UNROLL)

    @pl.when(i == pl.num_programs(0) - 1)
    def _():
        zero_row[...] = jnp.zeros_like(zero_row)

        def zero_copy(dst_row):
            return pltpu.make_async_copy(zero_row, rows_out.at[_tile_rows(dst_row)], zsem)

        def per_expert(e, n_pad):
            lo = pad_ref[0, e]
            hi = pad_ref[1, e]

            def body(r, carry):
                zero_copy(r).start()
                return carry

            lax.fori_loop(lo, hi, body, 0)
            return n_pad + (hi - lo)

        n_pad = lax.fori_loop(0, pad_ref.shape[1], per_expert, 0)

        def zdrain(r, carry):
            zero_copy(0).wait()
            return carry

        lax.fori_loop(0, n_pad, zdrain, 0)

    lax.fori_loop(0, n_assign, drain, 0, unroll=DMA_UNROLL)


def _k_major_tiles(dest, tt):
    t = dest.shape[0]
    return dest.reshape(t // tt, tt, TOP_K).transpose(0, 2, 1).reshape(t // tt, 1, TOP_K * tt)


def _moe_scatter(x_rows_src, dest, pad_ranges, n_rows, *, tt):
    t = x_rows_src.shape[0] // ROW_TILE
    dest2 = _k_major_tiles(dest, tt)
    return pl.pallas_call(
        _scatter_kernel,
        grid=(t // tt,),
        in_specs=[
            pl.BlockSpec((None, 1, tt * TOP_K), lambda i: (i, 0, 0), memory_space=pltpu.SMEM),
            pl.BlockSpec(pad_ranges.shape, lambda i: (0, 0), memory_space=pltpu.SMEM),
            pl.BlockSpec((tt * ROW_TILE, LANES), lambda i: (i, 0)),
        ],
        out_specs=pl.BlockSpec(memory_space=pl.ANY),
        out_shape=jax.ShapeDtypeStruct((n_rows * ROW_TILE, LANES), F32),
        scratch_shapes=[pltpu.VMEM((ROW_TILE, LANES), F32), pltpu.SemaphoreType.DMA(()),
                        pltpu.SemaphoreType.DMA(())],
        compiler_params=_params(("arbitrary",)),
        name="moe_scatter",
    )(dest2, pad_ranges, x_rows_src)


def _ffn_kernel(be_ref, nu_ref, x_ref, wgu_ref, bgu_ref, wdn_ref, bdn_ref, y_ref, wgu_bf, wdn_bf, x_bf):
    b = pl.program_id(0)
    prev = be_ref[jnp.maximum(b - 1, 0)]
    changed = jnp.logical_or(b == 0, be_ref[b] != prev)
    bm, d_model = x_bf.shape
    n_lane_tiles = d_model // LANES

    @pl.when(changed)
    def _():
        wgu_bf[...] = wgu_ref[...].astype(BF16)
        wdn_bf[...] = wdn_ref[...].astype(BF16)

    @pl.when(b < nu_ref[0])
    def _():
        de = wdn_bf.shape[0]
        for c in range(n_lane_tiles):
            x_bf[:, c * LANES:(c + 1) * LANES] = x_ref[pl.ds(c, bm, stride=ROW_TILE), :].astype(BF16)
        h = jnp.dot(x_bf[...], wgu_bf[...], preferred_element_type=F32) + bgu_ref[...]
        g = jnp.minimum(h[:, :de], SWIGLU_LIMIT)
        up = jnp.clip(h[:, de:], -SWIGLU_LIMIT, SWIGLU_LIMIT)
        act = (up + 1.0) * (g * jax.nn.sigmoid(SWIGLU_ALPHA * g))
        y = jnp.dot(act.astype(BF16), wdn_bf[...], preferred_element_type=F32) + bdn_ref[...]
        for c in range(n_lane_tiles):
            y_ref[pl.ds(c, bm, stride=ROW_TILE), :] = y[:, c * LANES:(c + 1) * LANES]

    @pl.when(b >= nu_ref[0])
    def _():
        y_ref[...] = jnp.zeros_like(y_ref)


def _moe_ffn(x_rows, block_e, n_used, w_gu, b_gu, w_dn, b_dn, layer, *, bm):
    n_rows = x_rows.shape[0] // ROW_TILE
    d, de = w_dn.shape[3], w_dn.shape[2]
    grid_spec = pltpu.PrefetchScalarGridSpec(
        num_scalar_prefetch=2,
        grid=(n_rows // bm,),
        in_specs=[
            pl.BlockSpec((bm * ROW_TILE, LANES), lambda b, be, nu: (jnp.minimum(b, nu[0] - 1), 0)),
            pl.BlockSpec((None, None, d, 2 * de), lambda b, be, nu: (layer, be[b], 0, 0)),
            pl.BlockSpec((None, None, 1, 2 * de), lambda b, be, nu: (layer, be[b], 0, 0)),
            pl.BlockSpec((None, None, de, d), lambda b, be, nu: (layer, be[b], 0, 0)),
            pl.BlockSpec((None, None, 1, d), lambda b, be, nu: (layer, be[b], 0, 0)),
        ],
        out_specs=pl.BlockSpec((bm * ROW_TILE, LANES), lambda b, be, nu: (b, 0)),
        scratch_shapes=[pltpu.VMEM((d, 2 * de), BF16), pltpu.VMEM((de, d), BF16),
                        pltpu.VMEM((bm, d), BF16)],
    )
    return pl.pallas_call(
        _ffn_kernel,
        grid_spec=grid_spec,
        out_shape=jax.ShapeDtypeStruct((n_rows * ROW_TILE, LANES), F32),
        compiler_params=_params(("arbitrary",)),
        name="moe_ffn",
    )(block_e, n_used, x_rows, w_gu, b_gu, w_dn, b_dn)


def _combine_kernel(dcur_ref, dnxt_ref, gate_ref, x_ref, g_ref, beta_ref, yrows, out_ref, buf, sem):
    i = pl.program_id(0)
    n_steps = pl.num_programs(0)
    n_assign = dcur_ref.shape[1]
    tt = n_assign // TOP_K

    def row_copy(dref, k, t, slot):
        return pltpu.make_async_copy(yrows.at[_tile_rows(dref[0, k * tt + t])],
                                     buf.at[slot, k, _tile_rows(t)], sem.at[slot])

    def issue(dref, slot):
        for k in range(TOP_K):
            def body(t, carry, k=k):
                row_copy(dref, k, t, slot).start()
                return carry
            lax.fori_loop(0, tt, body, 0, unroll=DMA_UNROLL)

    @pl.when(i == 0)
    def _():
        issue(dcur_ref, 0)

    @pl.when(i + 1 < n_steps)
    def _():
        issue(dnxt_ref, (i + 1) % 2)

    slot = i % 2

    def drain(j, carry):
        row_copy(dcur_ref, 0, 0, slot).wait()
        return carry

    lax.fori_loop(0, n_assign, drain, 0, unroll=DMA_UNROLL)

    gates = gate_ref[...]
    d_model = x_ref.shape[1]
    pieces = []
    for c in range(d_model // LANES):
        f = gates[:, 0:1] * buf.at[slot, 0][pl.ds(c, tt, stride=ROW_TILE), :]
        for k in range(1, TOP_K):
            f = f + gates[:, k:k + 1] * buf.at[slot, k][pl.ds(c, tt, stride=ROW_TILE), :]
        pieces.append(DEEPNORM_ALPHA * x_ref[:, c * LANES:(c + 1) * LANES] + f)
    out_ref[...] = _layer_norm(jnp.concatenate(pieces, axis=1), g_ref[...], beta_ref[...])


def _moe_combine(y_rows, dest, gate_w, x1, g, beta, *, tt):
    t, d = x1.shape
    n_steps = t // tt
    dest2 = _k_major_tiles(dest, tt)
    return pl.pallas_call(
        _combine_kernel,
        grid=(n_steps,),
        in_specs=[
            pl.BlockSpec((None, 1, tt * TOP_K), lambda i: (i, 0, 0), memory_space=pltpu.SMEM),
            pl.BlockSpec((None, 1, tt * TOP_K), lambda i: (jnp.minimum(i + 1, n_steps - 1), 0, 0),
                         memory_space=pltpu.SMEM),
            pl.BlockSpec((tt, LANES), lambda i: (i, 0)),
            pl.BlockSpec((tt, d), lambda i: (i, 0)),
            pl.BlockSpec((1, d), lambda i: (0, 0)),
            pl.BlockSpec((1, d), lambda i: (0, 0)),
            pl.BlockSpec(memory_space=pl.ANY),
        ],
        out_specs=pl.BlockSpec((tt, d), lambda i: (i, 0)),
        out_shape=jax.ShapeDtypeStruct((t, d), F32),
        scratch_shapes=[pltpu.VMEM((2, TOP_K, tt * ROW_TILE, LANES), F32), pltpu.SemaphoreType.DMA((2,))],
        compiler_params=_params(("arbitrary",)),
        name="moe_combine",
    )(dest2, dest2, gate_w, x1, g, beta, y_rows)


def _moe_layer(x1, x1_rows, logits, p, layer, *, bm, tt_route, tt_scatter, tt_combine):
    t, d = x1.shape
    ne = logits.shape[1]
    eidx_w, rank_w, gate_w, cnt = _route(logits, tt=tt_route)
    cnt = cnt[0]
    padded = (cnt + bm - 1) // bm * bm
    pend = jnp.cumsum(padded)
    pstart = pend - padded
    onehot_start = jnp.where(eidx_w[:, :TOP_K, None] == jnp.arange(ne, dtype=jnp.int32), pstart, 0)
    dest = (jnp.sum(onehot_start, axis=-1) + rank_w[:, :TOP_K]).astype(jnp.int32)
    n_blocks = (t * TOP_K) // bm + ne
    n_used = pend[-1] // bm
    blk = jnp.arange(n_blocks, dtype=jnp.int32)
    be = jnp.minimum(jnp.sum(pend[None, :] <= blk[:, None] * bm, axis=1), ne - 1).astype(jnp.int32)
    last_e = jnp.sum(jnp.where(blk == n_used - 1, be, 0))
    be = jnp.where(blk < n_used, be, last_e)
    pad_lo = jnp.concatenate([pstart + cnt, pend[-1:]])
    pad_hi = jnp.concatenate([pend, jnp.full((1,), n_blocks * bm, pend.dtype)])
    pad_ranges = jnp.stack([pad_lo, pad_hi]).astype(jnp.int32)
    x_rows = _moe_scatter(x1_rows, dest, pad_ranges, n_blocks * bm, tt=tt_scatter)
    y_rows = _moe_ffn(x_rows, be, n_used.reshape(1).astype(jnp.int32), p["moe_w_gu"], p["moe_b_gu4"],
                      p["moe_w_dn"], p["moe_b_dn4"], layer, bm=bm)
    return _moe_combine(y_rows, dest, gate_w, x1, p["ln_ffn_g"][layer][None],
                        p["ln_ffn_b"][layer][None], tt=tt_combine)


def _latent_kernel(x_ref, wc_ref, g_ref, wr_ref, wrs_ref, cos_ref, sin_ref, *rest, with_kv):
    if with_kv:
        wuk_ref, wuv_ref, c_ref, kr_ref, kp_ref, vp_ref = rest
    else:
        c_ref, kr_ref = rest
    xb = x_ref[...].astype(BF16)
    kv = jnp.dot(xb, wc_ref[...], preferred_element_type=F32)
    c = _rms_norm(kv, g_ref[...])
    r = jnp.dot(xb, wr_ref[...], preferred_element_type=F32)
    rs = jnp.dot(xb, wrs_ref[...], preferred_element_type=F32)
    kr = r * cos_ref[...] + rs * sin_ref[...]
    c_ref[...] = c
    kr_ref[...] = kr
    if with_kv:
        cb = c.astype(BF16)
        kn = jnp.dot(cb, wuk_ref[...], preferred_element_type=F32)
        n_heads = kn.shape[1] // HEAD_PAD
        for h in range(n_heads):
            sl = slice(h * HEAD_PAD, (h + 1) * HEAD_PAD)
            kp_ref[:, sl] = (kn[:, sl] + kr).astype(BF16)
        vp_ref[...] = jnp.dot(cb, wuv_ref[...], preferred_element_type=F32).astype(BF16)


def _mla_latent(x, wc, g_kv, wr, wrs, cos_k, sin_k, wuk_p, wuv_p, *, tm, with_kv):
    t, d = x.shape
    kvl = wc.shape[1]
    n_tab = cos_k.shape[0] // tm
    in_specs = [
        pl.BlockSpec((tm, d), lambda i: (i, 0)),
        pl.BlockSpec((d, kvl), lambda i: (0, 0)),
        pl.BlockSpec((1, kvl), lambda i: (0, 0)),
        pl.BlockSpec((d, HEAD_PAD), lambda i: (0, 0)),
        pl.BlockSpec((d, HEAD_PAD), lambda i: (0, 0)),
        pl.BlockSpec((tm, HEAD_PAD), lambda i: (i % n_tab, 0)),
        pl.BlockSpec((tm, HEAD_PAD), lambda i: (i % n_tab, 0)),
    ]
    out_specs = [
        pl.BlockSpec((tm, kvl), lambda i: (i, 0)),
        pl.BlockSpec((tm, HEAD_PAD), lambda i: (i, 0)),
    ]
    out_shape = [jax.ShapeDtypeStruct((t, kvl), F32), jax.ShapeDtypeStruct((t, HEAD_PAD), F32)]
    args = [x, wc, g_kv, wr, wrs, cos_k, sin_k]
    if with_kv:
        hp = wuk_p.shape[1]
        in_specs += [pl.BlockSpec((kvl, hp), lambda i: (0, 0)), pl.BlockSpec((kvl, hp), lambda i: (0, 0))]
        out_specs += [pl.BlockSpec((tm, hp), lambda i: (i, 0)), pl.BlockSpec((tm, hp), lambda i: (i, 0))]
        out_shape += [jax.ShapeDtypeStruct((t, hp), BF16), jax.ShapeDtypeStruct((t, hp), BF16)]
        args += [wuk_p, wuv_p]
    return pl.pallas_call(
        functools.partial(_latent_kernel, with_kv=with_kv),
        grid=(t // tm,),
        in_specs=in_specs,
        out_specs=out_specs,
        out_shape=out_shape,
        compiler_params=_params(("arbitrary",)),
        name="mla_latent",
    )(*args)


def _query_kernel(x_ref, wqa_ref, g_ref, wqb_ref, wqs_ref, cos_ref, sin_ref, q_ref):
    xb = x_ref[...].astype(BF16)
    cq = _rms_norm(jnp.dot(xb, wqa_ref[...], preferred_element_type=F32), g_ref[...]).astype(BF16)
    q = jnp.dot(cq, wqb_ref[...], preferred_element_type=F32)
    qs = jnp.dot(cq, wqs_ref[...], preferred_element_type=F32)
    cos = cos_ref[...]
    sin = sin_ref[...]
    n_heads = q.shape[1] // HEAD_PAD
    for h in range(n_heads):
        sl = slice(h * HEAD_PAD, (h + 1) * HEAD_PAD)
        q_ref[:, sl] = (q[:, sl] * cos + qs[:, sl] * sin).astype(BF16)


def _mla_queries(x, wqa, g_q, wqb_p, wqb_s, cos_q, sin_q, *, tm):
    t, d = x.shape
    ql = wqa.shape[1]
    hp = wqb_p.shape[1]
    n_tab = cos_q.shape[0] // tm
    return pl.pallas_call(
        _query_kernel,
        grid=(t // tm,),
        in_specs=[
            pl.BlockSpec((tm, d), lambda i: (i, 0)),
            pl.BlockSpec((d, ql), lambda i: (0, 0)),
            pl.BlockSpec((1, ql), lambda i: (0, 0)),
            pl.BlockSpec((ql, hp), lambda i: (0, 0)),
            pl.BlockSpec((ql, hp), lambda i: (0, 0)),
            pl.BlockSpec((tm, HEAD_PAD), lambda i: (i % n_tab, 0)),
            pl.BlockSpec((tm, HEAD_PAD), lambda i: (i % n_tab, 0)),
        ],
        out_specs=pl.BlockSpec((tm, hp), lambda i: (i, 0)),
        out_shape=jax.ShapeDtypeStruct((t, hp), BF16),
        compiler_params=_params(("arbitrary",)),
        name="mla_queries",
    )(x, wqa, g_q, wqb_p, wqb_s, cos_q, sin_q)


def _flash_kernel(q_ref, k_ref, v_ref, o_ref, *, tq):
    seq = q_ref.shape[0]
    n_tiles = seq // tq
    row = lax.broadcasted_iota(jnp.int32, (tq, tq), 0)
    col = lax.broadcasted_iota(jnp.int32, (tq, tq), 1)
    for i in range(n_tiles):
        q = q_ref[i * tq:(i + 1) * tq, :]
        m = jnp.full((tq, 1), -jnp.inf, F32)
        l = jnp.zeros((tq, 1), F32)
        acc = jnp.zeros((tq, HEAD_PAD), F32)
        for j in range(i + 1):
            k = k_ref[j * tq:(j + 1) * tq, :]
            v = v_ref[j * tq:(j + 1) * tq, :]
            s = lax.dot_general(q, k, (((1,), (1,)), ((), ())), preferred_element_type=F32) * ATTN_SCALE
            if j == i:
                s = jnp.where(col <= row, s, -jnp.inf)
            m_new = jnp.maximum(m, jnp.max(s, axis=1, keepdims=True))
            alpha = jnp.exp(m - m_new)
            pr = jnp.exp(s - m_new)
            l = alpha * l + jnp.sum(pr, axis=1, keepdims=True)
            acc = alpha * acc + jnp.dot(pr.astype(BF16), v, preferred_element_type=F32)
            m = m_new
        o_ref[i * tq:(i + 1) * tq, :] = (acc / l).astype(o_ref.dtype)


def _flash_attention(qp, kp, vp, *, n_seq, tq):
    t, hp = qp.shape
    seq = t // n_seq
    n_heads = hp // HEAD_PAD
    q3 = qp.reshape(n_seq, seq, hp)
    k3 = kp.reshape(n_seq, seq, hp)
    v3 = vp.reshape(n_seq, seq, hp)
    spec = pl.BlockSpec((None, seq, HEAD_PAD), lambda n, h: (n, 0, h))
    out = pl.pallas_call(
        functools.partial(_flash_kernel, tq=tq),
        grid=(n_seq, n_heads),
        in_specs=[spec, spec, spec],
        out_specs=spec,
        out_shape=jax.ShapeDtypeStruct((n_seq, seq, hp), BF16),
        compiler_params=_params(("arbitrary", "arbitrary")),
        name="flash_attention",
    )(q3, k3, v3)
    return out.reshape(t, hp)


def _qlat_kernel(q_ref, m_ref, o_ref):
    o_ref[...] = jnp.dot(q_ref[...], m_ref[...], preferred_element_type=F32).astype(o_ref.dtype)


def _q_latent(qp, m_heads):
    t, hp = qp.shape
    n_heads, _, width = m_heads.shape
    return pl.pallas_call(
        _qlat_kernel,
        grid=(n_heads,),
        in_specs=[
            pl.BlockSpec((t, HEAD_PAD), lambda h: (0, h)),
            pl.BlockSpec((None, HEAD_PAD, width), lambda h: (h, 0, 0)),
        ],
        out_specs=pl.BlockSpec((None, t, width), lambda h: (h, 0, 0)),
        out_shape=jax.ShapeDtypeStruct((n_heads, t, width), BF16),
        compiler_params=_params(("arbitrary",)),
        name="q_latent",
    )(qp, m_heads)


def _paged_kernel(pt_ref, q_ref, cnew_ref, rnew_ref, ckv_hbm, kr_hbm, o_ref,
                  cbuf, rbuf, sem_c, sem_r, m_scr, l_scr, acc_scr,
                  *, n_chunks, pages_per_step, n_pages, dec_seq, kvl):
    s = pl.program_id(0)
    n_steps = pl.num_programs(0)
    j = s % n_chunks
    page = cbuf.shape[2]

    def copies(step, slot, p):
        pg = pt_ref[(step // n_chunks) * n_pages + (step % n_chunks) * pages_per_step + p]
        return (pltpu.make_async_copy(ckv_hbm.at[pg], cbuf.at[slot, p], sem_c.at[slot]),
                pltpu.make_async_copy(kr_hbm.at[pg], rbuf.at[slot, p], sem_r.at[slot]))

    def issue(step, slot):
        for p in range(pages_per_step):
            cc, cr = copies(step, slot, p)
            cc.start()
            cr.start()

    @pl.when(s == 0)
    def _():
        issue(0, 0)

    @pl.when(s + 1 < n_steps)
    def _():
        issue(s + 1, (s + 1) % 2)

    slot = s % 2
    for p in range(pages_per_step):
        cc, cr = copies(s, slot, p)
        cc.wait()
        cr.wait()

    @pl.when(j == 0)
    def _():
        m_scr[...] = jnp.full_like(m_scr, -jnp.inf)
        l_scr[...] = jnp.zeros_like(l_scr)
        acc_scr[...] = jnp.zeros_like(acc_scr)

    q = q_ref[...]
    q_lat = q[:, :kvl]
    q_pe = q[:, kvl:kvl + QK_ROPE]
    nt = (((1,), (1,)), ((), ()))

    def update(c_b, r_b, mask):
        sc = (lax.dot_general(q_lat, c_b, nt, preferred_element_type=F32)
              + lax.dot_general(q_pe, r_b, nt, preferred_element_type=F32)) * ATTN_SCALE
        if mask is not None:
            sc = jnp.where(mask, sc, -jnp.inf)
        m_old = m_scr[...]
        m_new = jnp.maximum(m_old, jnp.max(sc, axis=1, keepdims=True))
        alpha = jnp.exp(m_old - m_new)
        pr = jnp.exp(sc - m_new)
        l_scr[...] = alpha * l_scr[...] + jnp.sum(pr, axis=1, keepdims=True)
        acc_scr[...] = alpha * acc_scr[...] + jnp.dot(pr.astype(BF16), c_b, preferred_element_type=F32)
        m_scr[...] = m_new

    c_b = cbuf[slot].reshape(pages_per_step * page, kvl).astype(BF16)
    r_b = rbuf[slot].reshape(pages_per_step * page, QK_ROPE).astype(BF16)
    update(c_b, r_b, None)

    @pl.when(j == n_chunks - 1)
    def _():
        rows = q.shape[0]
        n_new = cnew_ref.shape[0]
        q_l = lax.broadcasted_iota(jnp.int32, (rows, n_new), 0) % dec_seq
        kk = lax.broadcasted_iota(jnp.int32, (rows, n_new), 1)
        update(cnew_ref[...].astype(BF16), rnew_ref[...].astype(BF16), kk <= q_l)
        o_ref[...] = acc_scr[...] / l_scr[...]


def _paged_attention(qcat, c_new, kr_new, cache_ckv, cache_krope, page_table, *, dec_seq, n_heads,
                     pages_per_step):
    n_dec, n_pages = page_table.shape
    page, kvl = cache_ckv.shape[1:]
    rows = n_heads * dec_seq
    width = qcat.shape[1]
    n_chunks = n_pages // pages_per_step
    n_new = c_new.shape[1]
    kern = functools.partial(_paged_kernel, n_chunks=n_chunks, pages_per_step=pages_per_step,
                             n_pages=n_pages, dec_seq=dec_seq, kvl=kvl)
    grid_spec = pltpu.PrefetchScalarGridSpec(
        num_scalar_prefetch=1,
        grid=(n_dec * n_chunks,),
        in_specs=[
            pl.BlockSpec((rows, width), lambda s, pt: (s // n_chunks, 0)),
            pl.BlockSpec((None, n_new, kvl), lambda s, pt: (s // n_chunks, 0, 0)),
            pl.BlockSpec((None, n_new, QK_ROPE), lambda s, pt: (s // n_chunks, 0, 0)),
            pl.BlockSpec(memory_space=pl.ANY),
            pl.BlockSpec(memory_space=pl.ANY),
        ],
        out_specs=pl.BlockSpec((rows, kvl), lambda s, pt: (s // n_chunks, 0)),
        scratch_shapes=[
            pltpu.VMEM((2, pages_per_step, page, kvl), F32),
            pltpu.VMEM((2, pages_per_step, page, QK_ROPE), F32),
            pltpu.SemaphoreType.DMA((2,)),
            pltpu.SemaphoreType.DMA((2,)),
            pltpu.VMEM((rows, 1), F32),
            pltpu.VMEM((rows, 1), F32),
            pltpu.VMEM((rows, kvl), F32),
        ],
    )
    return pl.pallas_call(
        kern,
        grid_spec=grid_spec,
        out_shape=jax.ShapeDtypeStruct((n_dec * rows, kvl), F32),
        compiler_params=_params(("arbitrary",)),
        name="paged_attention",
    )(page_table.reshape(-1), qcat, c_new, kr_new, cache_ckv, cache_krope)


def _ov_kernel(o_ref, w_ref, out_ref):
    out_ref[...] = jnp.dot(o_ref[...].astype(BF16), w_ref[...], preferred_element_type=F32)


def _value_up(o_lat_hm, w_uv_hm):
    n_heads, t, kvl = o_lat_hm.shape
    vh = w_uv_hm.shape[2]
    return pl.pallas_call(
        _ov_kernel,
        grid=(n_heads,),
        in_specs=[
            pl.BlockSpec((None, t, kvl), lambda h: (h, 0, 0)),
            pl.BlockSpec((None, kvl, vh), lambda h: (h, 0, 0)),
        ],
        out_specs=pl.BlockSpec((None, t, vh), lambda h: (h, 0, 0)),
        out_shape=jax.ShapeDtypeStruct((n_heads, t, vh), F32),
        compiler_params=_params(("arbitrary",)),
        name="value_up",
    )(o_lat_hm, w_uv_hm)


def _s5_weights(lam_re, lam_im, log_dt, b_re, b_im, c_re, c_im):
    lr = lam_re.astype(F32)
    li = lam_im.astype(F32)
    dt = jnp.exp(log_dt.astype(F32))[:, None]
    mag = jnp.exp(lr * dt)
    ang = li * dt
    ab_re = mag * jnp.cos(ang)
    ab_im = mag * jnp.sin(ang)
    den = lr * lr + li * li
    f_re = ((ab_re - 1.0) * lr + ab_im * li) / den
    f_im = (ab_im * lr - (ab_re - 1.0) * li) / den
    br = b_re.astype(F32)
    bi = b_im.astype(F32)
    bb_re = f_re[..., None] * br - f_im[..., None] * bi
    bb_im = f_re[..., None] * bi + f_im[..., None] * br
    g = lr.shape[0]
    gpc = S5_GROUPS_PER_CHUNK
    n_chunks = g // gpc
    eye = jnp.eye(gpc, dtype=F32)

    def b_block(bb):
        t = bb.reshape(n_chunks, gpc, SSM_STATE, SSM_GROUP).transpose(0, 1, 3, 2)
        blk = t[:, :, :, None, :] * eye[None, :, None, :, None]
        return blk.reshape(n_chunks, gpc * SSM_GROUP, gpc * SSM_STATE)

    def c_block(cc):
        t = cc.reshape(n_chunks, gpc, SSM_GROUP, SSM_STATE).transpose(0, 1, 3, 2)
        blk = t[:, :, :, None, :] * eye[None, :, None, :, None]
        return blk.reshape(n_chunks, gpc * SSM_STATE, gpc * SSM_GROUP)

    bblk = jnp.concatenate([b_block(bb_re), b_block(bb_im)], axis=2).astype(BF16)
    cblk = jnp.concatenate([c_block(c_re.astype(F32)), -c_block(c_im.astype(F32))], axis=1).astype(BF16)
    a_re = ab_re.reshape(n_chunks, 1, gpc * SSM_STATE)
    a_im = ab_im.reshape(n_chunks, 1, gpc * SSM_STATE)
    return bblk, cblk, a_re, a_im


def _rope_tables(pos, n_rep):
    inv = 1.0 / (ROPE_THETA ** (jnp.arange(0, QK_ROPE, 2, dtype=F32) / QK_ROPE))
    ang = pos.astype(F32)[:, None] * inv[None, :]
    cos = jnp.cos(ang)
    sin = jnp.sin(ang)
    length = pos.shape[0]
    zero_lo = jnp.zeros((length, QK_NOPE), F32)
    zero_hi = jnp.zeros((length, HEAD_PAD - QK_NOPE - QK_ROPE), F32)
    cos_k = jnp.concatenate([zero_lo, cos, cos, zero_hi], axis=1)
    sin_k = jnp.concatenate([zero_lo, sin, sin, zero_hi], axis=1)
    cos_q = jnp.concatenate([jnp.ones((length, QK_NOPE), F32), cos, cos, zero_hi], axis=1)
    if n_rep > 1:
        cos_k, sin_k, cos_q = (jnp.tile(a, (n_rep, 1)) for a in (cos_k, sin_k, cos_q))
    return cos_k, sin_k, cos_q


def _pad_heads(w, n_heads, width, offset=0):
    k = w.shape[0]
    w3 = w.reshape(k, n_heads, width)
    out = jnp.zeros((k, n_heads, HEAD_PAD), w.dtype).at[:, :, offset:offset + width].set(w3)
    return out.reshape(k, n_heads * HEAD_PAD)


def _swap_rope(w_rope):
    half = QK_ROPE // 2
    return jnp.concatenate([-w_rope[..., half:], w_rope[..., :half]], axis=-1)


def _mla_weights(p, b):
    n_heads = p["mla_w_uk"].shape[1]
    kvl = p["mla_w_uk"].shape[0]
    w_qb = p["mla_w_q_b"][b]
    ql = w_qb.shape[0]
    w3 = w_qb.reshape(ql, n_heads, QK_NOPE + QK_ROPE)
    zero_tail = jnp.zeros((ql, n_heads, HEAD_PAD - QK_NOPE - QK_ROPE), F32)
    wqb_p = jnp.concatenate([w3, zero_tail], axis=2).reshape(ql, n_heads * HEAD_PAD)
    wqb_s = jnp.concatenate([jnp.zeros((ql, n_heads, QK_NOPE), F32), _swap_rope(w3[..., QK_NOPE:]),
                             zero_tail], axis=2).reshape(ql, n_heads * HEAD_PAD)
    w_o = p["mla_w_o"][b]
    d_model = w_o.shape[1]
    w_o_p = jnp.zeros((n_heads, HEAD_PAD, d_model), F32).at[:, :V_HEAD].set(
        w_o.reshape(n_heads, V_HEAD, d_model)).reshape(n_heads * HEAD_PAD, d_model)
    del kvl
    return dict(wqa=p["mla_w_q_a"][b].astype(BF16), g_q=p["mla_g_q"][b][None],
                wqb_p=wqb_p.astype(BF16), wqb_s=wqb_s.astype(BF16),
                w_o=w_o.astype(BF16), w_o_p=w_o_p.astype(BF16))


def _shared_mla_weights(p):
    w_kv_a = p["mla_w_kv_a"]
    d_model = w_kv_a.shape[0]
    kvl, n_heads, _ = p["mla_w_uk"].shape
    w_rope = w_kv_a[:, kvl:]
    pad_lo = jnp.zeros((d_model, QK_NOPE), F32)
    pad_hi = jnp.zeros((d_model, HEAD_PAD - QK_NOPE - QK_ROPE), F32)
    wr = jnp.concatenate([pad_lo, w_rope, pad_hi], axis=1)
    wrs = jnp.concatenate([pad_lo, _swap_rope(w_rope), pad_hi], axis=1)
    wuk_p = _pad_heads(p["mla_w_uk"].reshape(kvl, n_heads * QK_NOPE), n_heads, QK_NOPE)
    wuv_p = _pad_heads(p["mla_w_uv"].reshape(kvl, n_heads * V_HEAD), n_heads, V_HEAD)
    width = kvl + LANES
    m_heads = jnp.zeros((n_heads, HEAD_PAD, width), F32)
    m_heads = m_heads.at[:, :QK_NOPE, :kvl].set(p["mla_w_uk"].transpose(1, 2, 0))
    m_heads = m_heads.at[:, QK_NOPE:QK_NOPE + QK_ROPE, kvl:kvl + QK_ROPE].set(
        jnp.broadcast_to(jnp.eye(QK_ROPE, dtype=F32), (n_heads, QK_ROPE, QK_ROPE)))
    return dict(wc=w_kv_a[:, :kvl].astype(BF16), g_kv=p["mla_g_kv"][None], wr=wr.astype(BF16),
                wrs=wrs.astype(BF16), wuk_p=wuk_p.astype(BF16), wuv_p=wuv_p.astype(BF16),
                m_heads=m_heads.astype(BF16), w_uv_hm=p["mla_w_uv"].transpose(1, 0, 2).astype(BF16))


def _tiles(t):
    big = t >= 4096
    return dict(tm=256 if big else min(t, 128), bm=256 if big else 128,
                tt_route=512 if big else min(t, 128), tt_scatter=256 if big else min(t, 128),
                tt_combine=128 if big else min(t, 128))


def _trunk(x, pos, h0, past, p, s5w, mla_shared, mla_layers):
    n_seq, seq, d_model = x.shape
    t = n_seq * seq
    cfg = _tiles(t)
    n_a = len(s5w)
    n_states = d_model // SSM_GROUP * SSM_STATE

    assert d_model == ROW_TILE * LANES, "MoE row buffers hold one (8, 128) tile per row"

    def moe(proj_out, layer):
        x1, x1_rows, logits = proj_out
        return _moe_layer(x1, x1_rows, logits, p, layer, bm=cfg["bm"], tt_route=cfg["tt_route"],
                          tt_scatter=cfg["tt_scatter"], tt_combine=cfg["tt_combine"])

    xt = x.transpose(1, 0, 2).reshape(t, d_model)
    lt = min(seq, 256)
    new_re, new_im = [], []
    for a in range(n_a):
        bblk, cblk, a_re, a_im = s5w[a]
        if h0 is None:
            h0r = jnp.zeros((n_seq, n_states), F32)
            h0i = jnp.zeros((n_seq, n_states), F32)
        else:
            h0r = h0[0][a].reshape(n_seq, n_states)
            h0i = h0[1][a].reshape(n_seq, n_states)
        z, hr, hi = _s5_scan(xt, bblk, cblk, a_re, a_im, p["ssm_d"][a][None], h0r, h0i,
                             n_seq=n_seq, lt=lt)
        new_re.append(hr.reshape(n_seq, d_model // SSM_GROUP, SSM_STATE))
        new_im.append(hi.reshape(n_seq, d_model // SSM_GROUP, SSM_STATE))
        proj = _proj_ln_router(z, p["ssm_w_glu_bf"][a], p["ssm_b_glu"][a][None], xt,
                               p["ln_mix_g"][a][None], p["ln_mix_b"][a][None],
                               p["moe_w_router"][a], p["moe_b_router"][a][None],
                               glu=True, tm=cfg["tm"])
        xt = moe(proj, a)

    xb = xt.reshape(seq, n_seq, d_model).transpose(1, 0, 2).reshape(t, d_model)
    tm = cfg["tm"]
    n_rep = 1 if seq >= tm else tm // seq
    cos_k, sin_k, cos_q = _rope_tables(pos, n_rep)
    prompt = past is None
    lat = _mla_latent(xb, mla_shared["wc"], mla_shared["g_kv"], mla_shared["wr"], mla_shared["wrs"],
                      cos_k, sin_k, mla_shared["wuk_p"], mla_shared["wuv_p"], tm=tm, with_kv=prompt)
    c_lat, kr128 = lat[0], lat[1]
    kr = kr128[:, QK_NOPE:QK_NOPE + QK_ROPE]
    kvl = c_lat.shape[1]
    n_heads = mla_shared["m_heads"].shape[0]
    zero_bias = jnp.zeros((1, d_model), F32)
    if not prompt:
        cache_ckv, cache_krope, page_table = past
        n_new = -(-seq // SUBLANES) * SUBLANES
        c_new = jnp.zeros((n_seq, n_new, kvl), F32).at[:, :seq].set(c_lat.reshape(n_seq, seq, kvl))
        kr_new = jnp.zeros((n_seq, n_new, QK_ROPE), F32).at[:, :seq].set(kr.reshape(n_seq, seq, QK_ROPE))
    for b, mw in enumerate(mla_layers):
        layer = n_a + b
        qp = _mla_queries(xb, mw["wqa"], mw["g_q"], mw["wqb_p"], mw["wqb_s"], cos_q, sin_k, tm=tm)
        if prompt:
            o = _flash_attention(qp, lat[2], lat[3], n_seq=n_seq, tq=min(seq, 512))
            w_o = mw["w_o_p"]
        else:
            qcat = _q_latent(qp, mla_shared["m_heads"])
            width = qcat.shape[2]
            qcat = qcat.reshape(n_heads, n_seq, seq, width).transpose(1, 0, 2, 3).reshape(
                n_seq * n_heads * seq, width)
            o_lat = _paged_attention(qcat, c_new, kr_new, cache_ckv, cache_krope, page_table,
                                     dec_seq=seq, n_heads=n_heads, pages_per_step=16)
            o_hm = o_lat.reshape(n_seq, n_heads, seq, kvl).transpose(1, 0, 2, 3).reshape(n_heads, t, kvl)
            o = _value_up(o_hm, mla_shared["w_uv_hm"])
            o = o.transpose(1, 0, 2).reshape(t, n_heads * V_HEAD)
            w_o = mw["w_o"]
        proj = _proj_ln_router(o, w_o, zero_bias, xb, p["ln_mix_g"][layer][None],
                               p["ln_mix_b"][layer][None], p["moe_w_router"][layer],
                               p["moe_b_router"][layer][None], glu=False, tm=tm)
        xb = moe(proj, layer)
    y = xb.reshape(n_seq, seq, d_model)
    return (y, jnp.stack(new_re), jnp.stack(new_im), c_lat.reshape(n_seq, seq, kvl),
            kr.reshape(n_seq, seq, QK_ROPE))


def kernel(x_prompt, x_sample, state_ssm_re, state_ssm_im, cache_ckv, cache_krope, page_table,
           ssm_lam_re, ssm_lam_im, ssm_log_dt, ssm_b_re, ssm_b_im, ssm_c_re, ssm_c_im, ssm_d,
           ssm_w_glu, ssm_b_glu, mla_w_kv_a, mla_g_kv, mla_w_uk, mla_w_uv, mla_w_q_a, mla_g_q,
           mla_w_q_b, mla_w_o, moe_w_router, moe_b_router, moe_w_gu, moe_b_gu, moe_w_dn, moe_b_dn,
           ln_mix_g, ln_mix_b, ln_ffn_g, ln_ffn_b):
    p = dict(ssm_d=ssm_d, ssm_b_glu=ssm_b_glu, ssm_w_glu_bf=ssm_w_glu.astype(BF16),
             mla_w_kv_a=mla_w_kv_a, mla_g_kv=mla_g_kv, mla_w_uk=mla_w_uk, mla_w_uv=mla_w_uv,
             mla_w_q_a=mla_w_q_a, mla_g_q=mla_g_q, mla_w_q_b=mla_w_q_b, mla_w_o=mla_w_o,
             moe_w_router=moe_w_router, moe_b_router=moe_b_router, moe_w_gu=moe_w_gu,
             moe_b_gu4=moe_b_gu[:, :, None, :], moe_w_dn=moe_w_dn, moe_b_dn4=moe_b_dn[:, :, None, :],
             ln_mix_g=ln_mix_g, ln_mix_b=ln_mix_b, ln_ffn_g=ln_ffn_g, ln_ffn_b=ln_ffn_b)
    n_a = ssm_lam_re.shape[0]
    s5w = [_s5_weights(ssm_lam_re[a], ssm_lam_im[a], ssm_log_dt[a], ssm_b_re[a], ssm_b_im[a],
                       ssm_c_re[a], ssm_c_im[a]) for a in range(n_a)]
    mla_shared = _shared_mla_weights(p)
    mla_layers = [_mla_weights(p, b) for b in range(mla_w_q_a.shape[0])]

    n_pages = page_table.shape[1]
    past_len = n_pages * cache_ckv.shape[1]
    pos_sample = past_len + jnp.arange(x_sample.shape[1])
    out_s = _trunk(x_sample, pos_sample, (state_ssm_re, state_ssm_im),
                   (cache_ckv, cache_krope, page_table), p, s5w, mla_shared, mla_layers)
    pos_prompt = jnp.arange(x_prompt.shape[1])
    out_p = _trunk(x_prompt, pos_prompt, None, None, p, s5w, mla_shared, mla_layers)
    return (out_p[0], out_s[0], out_p[1], out_p[2], out_p[3], out_p[4],
            out_s[1], out_s[2], out_s[3], out_s[4])
```

```python
import functools
import math

import jax
import jax.numpy as jnp
from jax import lax
from jax.experimental import pallas as pl
from jax.experimental.pallas import tpu as pltpu

F32 = jnp.float32
BF16 = jnp.bfloat16

SSM_GROUP = 16
SSM_STATE = 64
QK_NOPE = 64
QK_ROPE = 32
V_HEAD = 64
ROPE_THETA = 10000.0
ATTN_SCALE = (QK_NOPE + QK_ROPE) ** -0.5
SCALE_LOG2E = ATTN_SCALE * math.log2(math.e)
TOP_K = 4
SWIGLU_LIMIT = 7.0
SWIGLU_ALPHA = 1.702
LN_EPS = 1e-5
RMS_EPS = 1e-6
DEPTH = 4
DEEPNORM_ALPHA = (2 * DEPTH) ** 0.25

LANES = 128
SUBLANES = 8
ROW_TILE = SUBLANES
HEAD_PAD = LANES
PAGES_PER_STEP = 64
DMA_UNROLL = 8
VMEM_LIMIT = 56 * 1024 * 1024

S5_CHUNK = LANES
S5_GROUPS_PER_CHUNK = S5_CHUNK // SSM_GROUP
S5_STATES_PER_CHUNK = S5_GROUPS_PER_CHUNK * SSM_STATE


def _params(sem, vmem=VMEM_LIMIT):
    return pltpu.CompilerParams(dimension_semantics=sem, vmem_limit_bytes=vmem)


def _layer_norm(x, g, b):
    mu = jnp.mean(x, axis=-1, keepdims=True)
    xc = x - mu
    var = jnp.mean(xc * xc, axis=-1, keepdims=True)
    return xc * lax.rsqrt(var + LN_EPS) * g + b


def _rms_norm(x, g):
    return x * lax.rsqrt(jnp.mean(x * x, axis=-1, keepdims=True) + RMS_EPS) * g


def _gelu_tanh(x):
    c = math.sqrt(2.0 / math.pi)
    return 0.5 * x * (1.0 + jnp.tanh(c * (x + 0.044715 * (x * x * x))))


def _s5_kernel(x_ref, bblk_ref, cblk_ref, are_ref, aim_ref, d_ref, h0r_ref, h0i_ref,
               z_ref, hr_out, hi_out, hbuf, st_re, st_im, *, n_seq, lt):
    i = pl.program_id(1)
    ns = S5_STATES_PER_CHUNK

    @pl.when(i == 0)
    def _():
        st_re[...] = h0r_ref[...]
        st_im[...] = h0i_ref[...]

    u = x_ref[...]
    hbuf[...] = jnp.dot(u.astype(BF16), bblk_ref[...], preferred_element_type=F32)
    a_re = jnp.broadcast_to(are_ref[...], (n_seq, ns))
    a_im = jnp.broadcast_to(aim_ref[...], (n_seq, ns))

    def step(l, carry):
        h_re, h_im = carry
        r0 = pl.multiple_of(l * n_seq, n_seq)
        bu_re = hbuf[pl.ds(r0, n_seq), 0:ns]
        bu_im = hbuf[pl.ds(r0, n_seq), ns:2 * ns]
        n_re = a_re * h_re - a_im * h_im + bu_re
        n_im = a_re * h_im + a_im * h_re + bu_im
        hbuf[pl.ds(r0, n_seq), 0:ns] = n_re
        hbuf[pl.ds(r0, n_seq), ns:2 * ns] = n_im
        return n_re, n_im

    h_re, h_im = lax.fori_loop(0, lt, step, (st_re[...], st_im[...]), unroll=min(lt, 8))
    st_re[...] = h_re
    st_im[...] = h_im

    y = jnp.dot(hbuf[...].astype(BF16), cblk_ref[...], preferred_element_type=F32)
    z_ref[...] = _gelu_tanh(y + d_ref[...] * u)

    @pl.when(i == pl.num_programs(1) - 1)
    def _():
        hr_out[...] = h_re
        hi_out[...] = h_im


def _s5_scan(x_tm, bblk, cblk, a_re, a_im, d_skip, h0_re, h0_im, *, n_seq, lt):
    rows, d_model = x_tm.shape
    n_chunks = d_model // S5_CHUNK
    ns = S5_STATES_PER_CHUNK
    seq = rows // n_seq
    tile = lt * n_seq
    kern = functools.partial(_s5_kernel, n_seq=n_seq, lt=lt)
    return pl.pallas_call(
        kern,
        grid=(n_chunks, seq // lt),
        in_specs=[
            pl.BlockSpec((tile, S5_CHUNK), lambda c, i: (i, c)),
            pl.BlockSpec((None, S5_CHUNK, 2 * ns), lambda c, i: (c, 0, 0)),
            pl.BlockSpec((None, 2 * ns, S5_CHUNK), lambda c, i: (c, 0, 0)),
            pl.BlockSpec((None, 1, ns), lambda c, i: (c, 0, 0)),
            pl.BlockSpec((None, 1, ns), lambda c, i: (c, 0, 0)),
            pl.BlockSpec((1, S5_CHUNK), lambda c, i: (0, c)),
            pl.BlockSpec((n_seq, ns), lambda c, i: (0, c)),
            pl.BlockSpec((n_seq, ns), lambda c, i: (0, c)),
        ],
        out_specs=[
            pl.BlockSpec((tile, S5_CHUNK), lambda c, i: (i, c)),
            pl.BlockSpec((n_seq, ns), lambda c, i: (0, c)),
            pl.BlockSpec((n_seq, ns), lambda c, i: (0, c)),
        ],
        out_shape=[
            jax.ShapeDtypeStruct((rows, d_model), F32),
            jax.ShapeDtypeStruct((n_seq, n_chunks * ns), F32),
            jax.ShapeDtypeStruct((n_seq, n_chunks * ns), F32),
        ],
        scratch_shapes=[
            pltpu.VMEM((tile, 2 * ns), F32),
            pltpu.VMEM((n_seq, ns), F32),
            pltpu.VMEM((n_seq, ns), F32),
        ],
        compiler_params=_params(("arbitrary", "arbitrary")),
        name="s5_scan",
    )(x_tm, bblk, cblk, a_re, a_im, d_skip, h0_re, h0_im)


def _split_bf16(v):
    hi = v.astype(BF16)
    return hi, (v - hi.astype(F32)).astype(BF16)


def _proj_ln_router_kernel(a_ref, w_ref, b_ref, x_ref, g_ref, beta_ref, wrh_ref, wrl_ref, br_ref,
                           x1_ref, xrow_ref, lg_ref, *, glu):
    h = jnp.dot(a_ref[...].astype(BF16), w_ref[...], preferred_element_type=F32) + b_ref[...]
    if glu:
        d = h.shape[1] // 2
        mix = h[:, :d] * jax.nn.sigmoid(h[:, d:])
    else:
        mix = h
    x1 = _layer_norm(DEEPNORM_ALPHA * x_ref[...] + mix, g_ref[...], beta_ref[...])
    x1_ref[...] = x1
    tm, d_model = x1.shape
    for c in range(d_model // LANES):
        xrow_ref[pl.ds(c, tm, stride=ROW_TILE), :] = x1[:, c * LANES:(c + 1) * LANES]
    xh, xl = _split_bf16(x1)
    lg_ref[...] = (jnp.dot(xh, wrh_ref[...], preferred_element_type=F32)
                   + jnp.dot(xl, wrh_ref[...], preferred_element_type=F32)
                   + jnp.dot(xh, wrl_ref[...], preferred_element_type=F32)) + br_ref[...]


def _proj_ln_router(a, w, b, x, g, beta, wr, br, *, glu, tm):
    t, k = a.shape
    nw = w.shape[1]
    d = x.shape[1]
    ne = wr.shape[1]
    wrh, wrl = _split_bf16(wr.astype(F32))
    kern = functools.partial(_proj_ln_router_kernel, glu=glu)
    return pl.pallas_call(
        kern,
        grid=(t // tm,),
        in_specs=[
            pl.BlockSpec((tm, k), lambda i: (i, 0)),
            pl.BlockSpec((k, nw), lambda i: (0, 0)),
            pl.BlockSpec((1, nw), lambda i: (0, 0)),
            pl.BlockSpec((tm, d), lambda i: (i, 0)),
            pl.BlockSpec((1, d), lambda i: (0, 0)),
            pl.BlockSpec((1, d), lambda i: (0, 0)),
            pl.BlockSpec((d, ne), lambda i: (0, 0)),
            pl.BlockSpec((d, ne), lambda i: (0, 0)),
            pl.BlockSpec((1, ne), lambda i: (0, 0)),
        ],
        out_specs=[
            pl.BlockSpec((tm, d), lambda i: (i, 0)),
            pl.BlockSpec((tm * ROW_TILE, LANES), lambda i: (i, 0)),
            pl.BlockSpec((tm, ne), lambda i: (i, 0)),
        ],
        out_shape=[
            jax.ShapeDtypeStruct((t, d), F32),
            jax.ShapeDtypeStruct((t * ROW_TILE, LANES), F32),
            jax.ShapeDtypeStruct((t, ne), F32),
        ],
        compiler_params=_params(("arbitrary",)),
        name="proj_ln_router",
    )(a, w, b, x, g, beta, wrh, wrl, br)


def _route_kernel(lg_ref, tri_ref, eidx_ref, rank_ref, gate_ref, cnt_ref, base_scr):
    i = pl.program_id(0)

    @pl.when(i == 0)
    def _():
        base_scr[...] = jnp.zeros_like(base_scr)

    l = lg_ref[...]
    tt, ne = l.shape
    lane = lax.broadcasted_iota(jnp.int32, (tt, ne), 1).astype(F32)
    tops, idxs, hots = [], [], []
    for _ in range(TOP_K):
        m = jnp.max(l, axis=1, keepdims=True)
        idx = jnp.min(jnp.where(l == m, lane, float(ne)), axis=1, keepdims=True)
        hot = lane == idx
        tops.append(m)
        idxs.append(idx.astype(jnp.int32))
        hots.append(hot)
        l = jnp.where(hot, -jnp.inf, l)
    sel = jnp.zeros((tt, ne), F32)
    for hot in hots:
        sel = sel + jnp.where(hot, 1.0, 0.0)
    prefix = jnp.dot(tri_ref[...], sel.astype(BF16), preferred_element_type=F32) + base_scr[...]
    es = [jnp.exp(m - tops[0]) for m in tops]
    den = es[0] + es[1] + es[2] + es[3]
    wide = lax.broadcasted_iota(jnp.int32, (tt, LANES), 1)
    eidx_w = jnp.zeros((tt, LANES), jnp.int32)
    rank_w = jnp.zeros((tt, LANES), jnp.int32)
    gate_w = jnp.zeros((tt, LANES), F32)
    for k in range(TOP_K):
        rank_k = jnp.sum(jnp.where(hots[k], prefix, 0.0), axis=1, keepdims=True).astype(jnp.int32)
        eidx_w = jnp.where(wide == k, idxs[k], eidx_w)
        rank_w = jnp.where(wide == k, rank_k, rank_w)
        gate_w = jnp.where(wide == k, es[k] / den, gate_w)
    eidx_ref[...] = eidx_w
    rank_ref[...] = rank_w
    gate_ref[...] = gate_w
    base_scr[...] = base_scr[...] + jnp.sum(sel, axis=0, keepdims=True)
    cnt_ref[...] = base_scr[...].astype(jnp.int32)


def _route(logits, *, tt):
    t, ne = logits.shape
    tri = jnp.tri(tt, k=-1, dtype=BF16)
    return pl.pallas_call(
        _route_kernel,
        grid=(t // tt,),
        in_specs=[
            pl.BlockSpec((tt, ne), lambda i: (i, 0)),
            pl.BlockSpec((tt, tt), lambda i: (0, 0)),
        ],
        out_specs=[
            pl.BlockSpec((tt, LANES), lambda i: (i, 0)),
            pl.BlockSpec((tt, LANES), lambda i: (i, 0)),
            pl.BlockSpec((tt, LANES), lambda i: (i, 0)),
            pl.BlockSpec((1, ne), lambda i: (0, 0)),
        ],
        out_shape=[
            jax.ShapeDtypeStruct((t, LANES), jnp.int32),
            jax.ShapeDtypeStruct((t, LANES), jnp.int32),
            jax.ShapeDtypeStruct((t, LANES), F32),
            jax.ShapeDtypeStruct((1, ne), jnp.int32),
        ],
        scratch_shapes=[pltpu.VMEM((1, ne), F32)],
        compiler_params=_params(("arbitrary",)),
        name="moe_route",
    )(logits, tri)


def _tile_rows(row):
    return pl.ds(pl.multiple_of(row * ROW_TILE, ROW_TILE), ROW_TILE)


def _scatter_kernel(dest_ref, pad_ref, x_ref, rows_out, zero_row, sem, zsem):
    i = pl.program_id(0)
    n_assign = dest_ref.shape[1]
    tt = n_assign // TOP_K

    def row_copy(src_row, dst_row):
        return pltpu.make_async_copy(x_ref.at[_tile_rows(src_row)], rows_out.at[_tile_rows(dst_row)], sem)

    def drain(j, carry):
        row_copy(0, 0).wait()
        return carry

    for k in range(TOP_K):
        def issue(g, carry, k=k):
            for u in range(DMA_UNROLL):
                t = g * DMA_UNROLL + u
                row_copy(t, dest_ref[0, k * tt + t]).start(priority=u % 2)
            return carry
        lax.fori_loop(0, tt // DMA_UNROLL, issue, 0)

    @pl.when(i == pl.num_programs(0) - 1)
    def _():
        zero_row[...] = jnp.zeros_like(zero_row)

        def zero_copy(dst_row):
            return pltpu.make_async_copy(zero_row, rows_out.at[_tile_rows(dst_row)], zsem)

        def per_expert(e, n_pad):
            lo = pad_ref[0, e]
            hi = pad_ref[1, e]

            def body(r, carry):
                zero_copy(r).start()
                return carry

            lax.fori_loop(lo, hi, body, 0)
            return n_pad + (hi - lo)

        n_pad = lax.fori_loop(0, pad_ref.shape[1], per_expert, 0)

        def zdrain(r, carry):
            zero_copy(0).wait()
            return carry

        lax.fori_loop(0, n_pad, zdrain, 0)

    lax.fori_loop(0, n_assign, drain, 0, unroll=DMA_UNROLL)


def _k_major_tiles(dest, tt):
    t = dest.shape[0]
    return dest.reshape(t // tt, tt, TOP_K).transpose(0, 2, 1).reshape(t // tt, 1, TOP_K * tt)


def _moe_scatter(x_rows_src, dest, pad_ranges, n_rows, *, tt):
    t = x_rows_src.shape[0] // ROW_TILE
    dest2 = _k_major_tiles(dest, tt)
    return pl.pallas_call(
        _scatter_kernel,
        grid=(t // tt,),
        in_specs=[
            pl.BlockSpec((None, 1, tt * TOP_K), lambda i: (i, 0, 0), memory_space=pltpu.SMEM),
            pl.BlockSpec(pad_ranges.shape, lambda i: (0, 0), memory_space=pltpu.SMEM),
            pl.BlockSpec((tt * ROW_TILE, LANES), lambda i: (i, 0)),
        ],
        out_specs=pl.BlockSpec(memory_space=pl.ANY),
        out_shape=jax.ShapeDtypeStruct((n_rows * ROW_TILE, LANES), F32),
        scratch_shapes=[pltpu.VMEM((ROW_TILE, LANES), F32), pltpu.SemaphoreType.DMA(()),
                        pltpu.SemaphoreType.DMA(())],
        compiler_params=_params(("arbitrary",)),
        name="moe_scatter",
    )(dest2, pad_ranges, x_rows_src)


def _ffn_kernel(be_ref, nu_ref, x_ref, wgu_ref, bgu_ref, wdn_ref, bdn_ref, y_ref, wgu_bf, wdn_bf, x_bf):
    b = pl.program_id(0)
    prev = be_ref[jnp.maximum(b - 1, 0)]
    changed = jnp.logical_or(b == 0, be_ref[b] != prev)
    bm, d_model = x_bf.shape
    n_lane_tiles = d_model // LANES

    @pl.when(changed)
    def _():
        wgu_bf[...] = wgu_ref[...].astype(BF16)
        wdn_bf[...] = wdn_ref[...].astype(BF16)

    @pl.when(b < nu_ref[0])
    def _():
        de = wdn_bf.shape[0]
        for c in range(n_lane_tiles):
            x_bf[:, c * LANES:(c + 1) * LANES] = x_ref[pl.ds(c, bm, stride=ROW_TILE), :].astype(BF16)
        h = jnp.dot(x_bf[...], wgu_bf[...], preferred_element_type=F32) + bgu_ref[...]
        g = jnp.minimum(h[:, :de], SWIGLU_LIMIT)
        up = jnp.clip(h[:, de:], -SWIGLU_LIMIT, SWIGLU_LIMIT)
        act = (up + 1.0) * (g * jax.nn.sigmoid(SWIGLU_ALPHA * g))
        y = jnp.dot(act.astype(BF16), wdn_bf[...], preferred_element_type=F32) + bdn_ref[...]
        for c in range(n_lane_tiles):
            y_ref[pl.ds(c, bm, stride=ROW_TILE), :] = y[:, c * LANES:(c + 1) * LANES]

    @pl.when(b >= nu_ref[0])
    def _():
        y_ref[...] = jnp.zeros_like(y_ref)


def _moe_ffn(x_rows, block_e, n_used, w_gu, b_gu, w_dn, b_dn, layer, *, bm):
    n_rows = x_rows.shape[0] // ROW_TILE
    d, de = w_dn.shape[3], w_dn.shape[2]
    grid_spec = pltpu.PrefetchScalarGridSpec(
        num_scalar_prefetch=2,
        grid=(n_rows // bm,),
        in_specs=[
            pl.BlockSpec((bm * ROW_TILE, LANES), lambda b, be, nu: (jnp.minimum(b, nu[0] - 1), 0)),
            pl.BlockSpec((None, None, d, 2 * de), lambda b, be, nu: (layer, be[b], 0, 0)),
            pl.BlockSpec((None, None, 1, 2 * de), lambda b, be, nu: (layer, be[b], 0, 0)),
            pl.BlockSpec((None, None, de, d), lambda b, be, nu: (layer, be[b], 0, 0)),
            pl.BlockSpec((None, None, 1, d), lambda b, be, nu: (layer, be[b], 0, 0)),
        ],
        out_specs=pl.BlockSpec((bm * ROW_TILE, LANES), lambda b, be, nu: (b, 0)),
        scratch_shapes=[pltpu.VMEM((d, 2 * de), BF16), pltpu.VMEM((de, d), BF16),
                        pltpu.VMEM((bm, d), BF16)],
    )
    return pl.pallas_call(
        _ffn_kernel,
        grid_spec=grid_spec,
        out_shape=jax.ShapeDtypeStruct((n_rows * ROW_TILE, LANES), F32),
        compiler_params=_params(("arbitrary",)),
        name="moe_ffn",
    )(block_e, n_used, x_rows, w_gu, b_gu, w_dn, b_dn)


def _combine_kernel(dcur_ref, dnxt_ref, gate_ref, x_ref, g_ref, beta_ref, yrows, out_ref, buf, sem):
    i = pl.program_id(0)
    n_steps = pl.num_programs(0)
    n_assign = dcur_ref.shape[1]
    tt = n_assign // TOP_K

    def row_copy(dref, k, t, slot):
        return pltpu.make_async_copy(yrows.at[_tile_rows(dref[0, k * tt + t])],
                                     buf.at[slot, k, _tile_rows(t)], sem.at[slot])

    def issue(dref, slot):
        for k in range(TOP_K):
            def body(g, carry, k=k):
                for u in range(DMA_UNROLL):
                    row_copy(dref, k, g * DMA_UNROLL + u, slot).start(priority=u % 2)
                return carry
            lax.fori_loop(0, tt // DMA_UNROLL, body, 0)

    @pl.when(i == 0)
    def _():
        issue(dcur_ref, 0)

    @pl.when(i + 1 < n_steps)
    def _():
        issue(dnxt_ref, (i + 1) % 2)

    slot = i % 2

    def drain(j, carry):
        row_copy(dcur_ref, 0, 0, slot).wait()
        return carry

    lax.fori_loop(0, n_assign, drain, 0, unroll=DMA_UNROLL)

    gates = gate_ref[...]
    d_model = x_ref.shape[1]
    pieces = []
    for c in range(d_model // LANES):
        f = gates[:, 0:1] * buf.at[slot, 0][pl.ds(c, tt, stride=ROW_TILE), :]
        for k in range(1, TOP_K):
            f = f + gates[:, k:k + 1] * buf.at[slot, k][pl.ds(c, tt, stride=ROW_TILE), :]
        pieces.append(DEEPNORM_ALPHA * x_ref[:, c * LANES:(c + 1) * LANES] + f)
    out_ref[...] = _layer_norm(jnp.concatenate(pieces, axis=1), g_ref[...], beta_ref[...])


def _moe_combine(y_rows, dest, gate_w, x1, g, beta, *, tt):
    t, d = x1.shape
    n_steps = t // tt
    dest2 = _k_major_tiles(dest, tt)
    return pl.pallas_call(
        _combine_kernel,
        grid=(n_steps,),
        in_specs=[
            pl.BlockSpec((None, 1, tt * TOP_K), lambda i: (i, 0, 0), memory_space=pltpu.SMEM),
            pl.BlockSpec((None, 1, tt * TOP_K), lambda i: (jnp.minimum(i + 1, n_steps - 1), 0, 0),
                         memory_space=pltpu.SMEM),
            pl.BlockSpec((tt, LANES), lambda i: (i, 0)),
            pl.BlockSpec((tt, d), lambda i: (i, 0)),
            pl.BlockSpec((1, d), lambda i: (0, 0)),
            pl.BlockSpec((1, d), lambda i: (0, 0)),
            pl.BlockSpec(memory_space=pl.ANY),
        ],
        out_specs=pl.BlockSpec((tt, d), lambda i: (i, 0)),
        out_shape=jax.ShapeDtypeStruct((t, d), F32),
        scratch_shapes=[pltpu.VMEM((2, TOP_K, tt * ROW_TILE, LANES), F32), pltpu.SemaphoreType.DMA((2,))],
        compiler_params=_params(("arbitrary",)),
        name="moe_combine",
    )(dest2, dest2, gate_w, x1, g, beta, y_rows)


def _moe_layer(x1, x1_rows, logits, p, layer, *, bm, tt_route, tt_scatter, tt_combine):
    t, d = x1.shape
    ne = logits.shape[1]
    eidx_w, rank_w, gate_w, cnt = _route(logits, tt=tt_route)
    cnt = cnt[0]
    padded = (cnt + bm - 1) // bm * bm
    pend = jnp.cumsum(padded)
    pstart = pend - padded
    onehot_start = jnp.where(eidx_w[:, :TOP_K, None] == jnp.arange(ne, dtype=jnp.int32), pstart, 0)
    dest = (jnp.sum(onehot_start, axis=-1) + rank_w[:, :TOP_K]).astype(jnp.int32)
    n_blocks = (t * TOP_K) // bm + ne
    n_used = pend[-1] // bm
    blk = jnp.arange(n_blocks, dtype=jnp.int32)
    be = jnp.minimum(jnp.sum(pend[None, :] <= blk[:, None] * bm, axis=1), ne - 1).astype(jnp.int32)
    last_e = jnp.sum(jnp.where(blk == n_used - 1, be, 0))
    be = jnp.where(blk < n_used, be, last_e)
    pad_lo = jnp.concatenate([pstart + cnt, pend[-1:]])
    pad_hi = jnp.concatenate([pend, jnp.full((1,), n_blocks * bm, pend.dtype)])
    pad_ranges = jnp.stack([pad_lo, pad_hi]).astype(jnp.int32)
    x_rows = _moe_scatter(x1_rows, dest, pad_ranges, n_blocks * bm, tt=tt_scatter)
    y_rows = _moe_ffn(x_rows, be, n_used.reshape(1).astype(jnp.int32), p["moe_w_gu"], p["moe_b_gu4"],
                      p["moe_w_dn"], p["moe_b_dn4"], layer, bm=bm)
    return _moe_combine(y_rows, dest, gate_w, x1, p["ln_ffn_g"][layer][None],
                        p["ln_ffn_b"][layer][None], tt=tt_combine)


def _latent_kernel(x_ref, wc_ref, g_ref, wr_ref, wrs_ref, cos_ref, sin_ref, *rest, with_kv):
    if with_kv:
        wuk_ref, wuv_ref, c_ref, kr_ref, kp_ref, vp_ref = rest
    else:
        c_ref, kr_ref = rest
    xb = x_ref[...].astype(BF16)
    kv = jnp.dot(xb, wc_ref[...], preferred_element_type=F32)
    c = _rms_norm(kv, g_ref[...])
    r = jnp.dot(xb, wr_ref[...], preferred_element_type=F32)
    rs = jnp.dot(xb, wrs_ref[...], preferred_element_type=F32)
    kr = r * cos_ref[...] + rs * sin_ref[...]
    c_ref[...] = c
    kr_ref[...] = kr
    if with_kv:
        cb = c.astype(BF16)
        kn = jnp.dot(cb, wuk_ref[...], preferred_element_type=F32)
        n_heads = kn.shape[1] // HEAD_PAD
        for h in range(n_heads):
            sl = slice(h * HEAD_PAD, (h + 1) * HEAD_PAD)
            kp_ref[:, sl] = (kn[:, sl] + kr).astype(BF16)
        vp_ref[...] = jnp.dot(cb, wuv_ref[...], preferred_element_type=F32).astype(BF16)


def _mla_latent(x, wc, g_kv, wr, wrs, cos_k, sin_k, wuk_p, wuv_p, *, tm, with_kv):
    t, d = x.shape
    kvl = wc.shape[1]
    n_tab = cos_k.shape[0] // tm
    in_specs = [
        pl.BlockSpec((tm, d), lambda i: (i, 0)),
        pl.BlockSpec((d, kvl), lambda i: (0, 0)),
        pl.BlockSpec((1, kvl), lambda i: (0, 0)),
        pl.BlockSpec((d, HEAD_PAD), lambda i: (0, 0)),
        pl.BlockSpec((d, HEAD_PAD), lambda i: (0, 0)),
        pl.BlockSpec((tm, HEAD_PAD), lambda i: (i % n_tab, 0)),
        pl.BlockSpec((tm, HEAD_PAD), lambda i: (i % n_tab, 0)),
    ]
    out_specs = [
        pl.BlockSpec((tm, kvl), lambda i: (i, 0)),
        pl.BlockSpec((tm, HEAD_PAD), lambda i: (i, 0)),
    ]
    out_shape = [jax.ShapeDtypeStruct((t, kvl), F32), jax.ShapeDtypeStruct((t, HEAD_PAD), F32)]
    args = [x, wc, g_kv, wr, wrs, cos_k, sin_k]
    if with_kv:
        hp = wuk_p.shape[1]
        in_specs += [pl.BlockSpec((kvl, hp), lambda i: (0, 0)), pl.BlockSpec((kvl, hp), lambda i: (0, 0))]
        out_specs += [pl.BlockSpec((tm, hp), lambda i: (i, 0)), pl.BlockSpec((tm, hp), lambda i: (i, 0))]
        out_shape += [jax.ShapeDtypeStruct((t, hp), BF16), jax.ShapeDtypeStruct((t, hp), BF16)]
        args += [wuk_p, wuv_p]
    return pl.pallas_call(
        functools.partial(_latent_kernel, with_kv=with_kv),
        grid=(t // tm,),
        in_specs=in_specs,
        out_specs=out_specs,
        out_shape=out_shape,
        compiler_params=_params(("arbitrary",)),
        name="mla_latent",
    )(*args)


def _query_kernel(x_ref, wqa_ref, g_ref, wqb_ref, wqs_ref, cos_ref, sin_ref, q_ref):
    xb = x_ref[...].astype(BF16)
    cq = _rms_norm(jnp.dot(xb, wqa_ref[...], preferred_element_type=F32), g_ref[...]).astype(BF16)
    q = jnp.dot(cq, wqb_ref[...], preferred_element_type=F32)
    qs = jnp.dot(cq, wqs_ref[...], preferred_element_type=F32)
    cos = cos_ref[...]
    sin = sin_ref[...]
    n_heads = q.shape[1] // HEAD_PAD
    for h in range(n_heads):
        sl = slice(h * HEAD_PAD, (h + 1) * HEAD_PAD)
        q_ref[:, sl] = (q[:, sl] * cos + qs[:, sl] * sin).astype(BF16)


def _mla_queries(x, wqa, g_q, wqb_p, wqb_s, cos_q, sin_q, *, tm):
    t, d = x.shape
    ql = wqa.shape[1]
    hp = wqb_p.shape[1]
    n_tab = cos_q.shape[0] // tm
    return pl.pallas_call(
        _query_kernel,
        grid=(t // tm,),
        in_specs=[
            pl.BlockSpec((tm, d), lambda i: (i, 0)),
            pl.BlockSpec((d, ql), lambda i: (0, 0)),
            pl.BlockSpec((1, ql), lambda i: (0, 0)),
            pl.BlockSpec((ql, hp), lambda i: (0, 0)),
            pl.BlockSpec((ql, hp), lambda i: (0, 0)),
            pl.BlockSpec((tm, HEAD_PAD), lambda i: (i % n_tab, 0)),
            pl.BlockSpec((tm, HEAD_PAD), lambda i: (i % n_tab, 0)),
        ],
        out_specs=pl.BlockSpec((tm, hp), lambda i: (i, 0)),
        out_shape=jax.ShapeDtypeStruct((t, hp), BF16),
        compiler_params=_params(("arbitrary",)),
        name="mla_queries",
    )(x, wqa, g_q, wqb_p, wqb_s, cos_q, sin_q)


def _flash_kernel(q_ref, k_ref, v_ref, o_ref, *, tq):
    seq = q_ref.shape[0]
    n_tiles = seq // tq
    row = lax.broadcasted_iota(jnp.int32, (tq, tq), 0)
    col = lax.broadcasted_iota(jnp.int32, (tq, tq), 1)
    for i in range(n_tiles):
        q = q_ref[i * tq:(i + 1) * tq, :]
        m = jnp.full((tq, 1), -jnp.inf, F32)
        l = jnp.zeros((tq, 1), F32)
        acc = jnp.zeros((tq, HEAD_PAD), F32)
        for j in range(i + 1):
            k = k_ref[j * tq:(j + 1) * tq, :]
            v = v_ref[j * tq:(j + 1) * tq, :]
            s = lax.dot_general(q, k, (((1,), (1,)), ((), ())), preferred_element_type=F32)
            if j == i:
                s = jnp.where(col <= row, s, -jnp.inf)
            m_new = jnp.maximum(m, jnp.max(s, axis=1, keepdims=True))
            alpha = jnp.exp2((m - m_new) * SCALE_LOG2E)
            pr = jnp.exp2((s - m_new) * SCALE_LOG2E)
            l = alpha * l + jnp.sum(pr, axis=1, keepdims=True)
            acc = alpha * acc + jnp.dot(pr.astype(BF16), v, preferred_element_type=F32)
            m = m_new
        o_ref[i * tq:(i + 1) * tq, :] = (acc / l).astype(o_ref.dtype)


def _flash_attention(qp, kp, vp, *, n_seq, tq):
    t, hp = qp.shape
    seq = t // n_seq
    n_heads = hp // HEAD_PAD
    q3 = qp.reshape(n_seq, seq, hp)
    k3 = kp.reshape(n_seq, seq, hp)
    v3 = vp.reshape(n_seq, seq, hp)
    spec = pl.BlockSpec((None, seq, HEAD_PAD), lambda n, h: (n, 0, h))
    out = pl.pallas_call(
        functools.partial(_flash_kernel, tq=tq),
        grid=(n_seq, n_heads),
        in_specs=[spec, spec, spec],
        out_specs=spec,
        out_shape=jax.ShapeDtypeStruct((n_seq, seq, hp), BF16),
        compiler_params=_params(("arbitrary", "arbitrary")),
        name="flash_attention",
    )(q3, k3, v3)
    return out.reshape(t, hp)


def _qlat_kernel(q_ref, m_ref, o_ref):
    o_ref[...] = jnp.dot(q_ref[...], m_ref[...], preferred_element_type=F32).astype(o_ref.dtype)


def _q_latent(qp, m_heads):
    t, hp = qp.shape
    n_heads, _, width = m_heads.shape
    return pl.pallas_call(
        _qlat_kernel,
        grid=(n_heads,),
        in_specs=[
            pl.BlockSpec((t, HEAD_PAD), lambda h: (0, h)),
            pl.BlockSpec((None, HEAD_PAD, width), lambda h: (h, 0, 0)),
        ],
        out_specs=pl.BlockSpec((None, t, width), lambda h: (h, 0, 0)),
        out_shape=jax.ShapeDtypeStruct((n_heads, t, width), BF16),
        compiler_params=_params(("arbitrary",)),
        name="q_latent",
    )(qp, m_heads)


def _paged_kernel(pt_ref, q_ref, cnew_ref, rnew_ref, ckv_hbm, kr_hbm, o_ref,
                  cbuf, rbuf, sem_c, sem_r, m_scr, l_scr, acc_scr,
                  *, n_chunks, pages_per_step, n_pages, dec_seq, kvl):
    s = pl.program_id(0)
    n_steps = pl.num_programs(0)
    j = s % n_chunks
    page = cbuf.shape[2]

    def copies(step, slot, p):
        pg = pt_ref[(step // n_chunks) * n_pages + (step % n_chunks) * pages_per_step + p]
        return (pltpu.make_async_copy(ckv_hbm.at[pg], cbuf.at[slot, p], sem_c.at[slot]),
                pltpu.make_async_copy(kr_hbm.at[pg], rbuf.at[slot, p], sem_r.at[slot]))

    def issue(step, slot):
        for p in range(pages_per_step):
            cc, cr = copies(step, slot, p)
            cc.start()
            cr.start()

    @pl.when(s == 0)
    def _():
        issue(0, 0)

    @pl.when(s + 1 < n_steps)
    def _():
        issue(s + 1, (s + 1) % 2)

    slot = s % 2
    for p in range(pages_per_step):
        cc, cr = copies(s, slot, p)
        cc.wait()
        cr.wait()

    @pl.when(j == 0)
    def _():
        m_scr[...] = jnp.full_like(m_scr, -jnp.inf)
        l_scr[...] = jnp.zeros_like(l_scr)
        acc_scr[...] = jnp.zeros_like(acc_scr)

    q = q_ref[...]
    q_lat = q[:, :kvl]
    q_pe = q[:, kvl:kvl + QK_ROPE]
    nt = (((1,), (1,)), ((), ()))

    def update(c_b, r_b, mask):
        sc = (lax.dot_general(q_lat, c_b, nt, preferred_element_type=F32)
              + lax.dot_general(q_pe, r_b, nt, preferred_element_type=F32))
        if mask is not None:
            sc = jnp.where(mask, sc, -jnp.inf)
        m_old = m_scr[...]
        m_new = jnp.maximum(m_old, jnp.max(sc, axis=1, keepdims=True))
        alpha = jnp.exp2((m_old - m_new) * SCALE_LOG2E)
        pr = jnp.exp2((sc - m_new) * SCALE_LOG2E)
        l_scr[...] = alpha * l_scr[...] + jnp.sum(pr, axis=1, keepdims=True)
        acc_scr[...] = alpha * acc_scr[...] + jnp.dot(pr.astype(BF16), c_b, preferred_element_type=F32)
        m_scr[...] = m_new

    c_b = cbuf[slot].reshape(pages_per_step * page, kvl).astype(BF16)
    r_b = rbuf[slot].reshape(pages_per_step * page, QK_ROPE).astype(BF16)
    update(c_b, r_b, None)

    @pl.when(j == n_chunks - 1)
    def _():
        rows = q.shape[0]
        n_new = cnew_ref.shape[0]
        q_l = lax.broadcasted_iota(jnp.int32, (rows, n_new), 0) % dec_seq
        kk = lax.broadcasted_iota(jnp.int32, (rows, n_new), 1)
        update(cnew_ref[...].astype(BF16), rnew_ref[...].astype(BF16), kk <= q_l)
        o_ref[...] = acc_scr[...] / l_scr[...]


def _paged_attention(qcat, c_new, kr_new, cache_ckv, cache_krope, page_table, *, dec_seq, n_heads,
                     pages_per_step):
    n_dec, n_pages = page_table.shape
    page, kvl = cache_ckv.shape[1:]
    rows = n_heads * dec_seq
    width = qcat.shape[1]
    n_chunks = n_pages // pages_per_step
    n_new = c_new.shape[1]
    kern = functools.partial(_paged_kernel, n_chunks=n_chunks, pages_per_step=pages_per_step,
                             n_pages=n_pages, dec_seq=dec_seq, kvl=kvl)
    grid_spec = pltpu.PrefetchScalarGridSpec(
        num_scalar_prefetch=1,
        grid=(n_dec * n_chunks,),
        in_specs=[
            pl.BlockSpec((rows, width), lambda s, pt: (s // n_chunks, 0)),
            pl.BlockSpec((None, n_new, kvl), lambda s, pt: (s // n_chunks, 0, 0)),
            pl.BlockSpec((None, n_new, QK_ROPE), lambda s, pt: (s // n_chunks, 0, 0)),
            pl.BlockSpec(memory_space=pl.ANY),
            pl.BlockSpec(memory_space=pl.ANY),
        ],
        out_specs=pl.BlockSpec((rows, kvl), lambda s, pt: (s // n_chunks, 0)),
        scratch_shapes=[
            pltpu.VMEM((2, pages_per_step, page, kvl), F32),
            pltpu.VMEM((2, pages_per_step, page, QK_ROPE), F32),
            pltpu.SemaphoreType.DMA((2,)),
            pltpu.SemaphoreType.DMA((2,)),
            pltpu.VMEM((rows, 1), F32),
            pltpu.VMEM((rows, 1), F32),
            pltpu.VMEM((rows, kvl), F32),
        ],
    )
    return pl.pallas_call(
        kern,
        grid_spec=grid_spec,
        out_shape=jax.ShapeDtypeStruct((n_dec * rows, kvl), F32),
        compiler_params=_params(("arbitrary",)),
        name="paged_attention",
    )(page_table.reshape(-1), qcat, c_new, kr_new, cache_ckv, cache_krope)


def _ov_kernel(o_ref, w_ref, out_ref):
    out_ref[...] = jnp.dot(o_ref[...].astype(BF16), w_ref[...], preferred_element_type=F32)


def _value_up(o_lat_hm, w_uv_hm):
    n_heads, t, kvl = o_lat_hm.shape
    vh = w_uv_hm.shape[2]
    return pl.pallas_call(
        _ov_kernel,
        grid=(n_heads,),
        in_specs=[
            pl.BlockSpec((None, t, kvl), lambda h: (h, 0, 0)),
            pl.BlockSpec((None, kvl, vh), lambda h: (h, 0, 0)),
        ],
        out_specs=pl.BlockSpec((None, t, vh), lambda h: (h, 0, 0)),
        out_shape=jax.ShapeDtypeStruct((n_heads, t, vh), F32),
        compiler_params=_params(("arbitrary",)),
        name="value_up",
    )(o_lat_hm, w_uv_hm)


def _s5_weights(lam_re, lam_im, log_dt, b_re, b_im, c_re, c_im):
    lr = lam_re.astype(F32)
    li = lam_im.astype(F32)
    dt = jnp.exp(log_dt.astype(F32))[:, None]
    mag = jnp.exp(lr * dt)
    ang = li * dt
    ab_re = mag * jnp.cos(ang)
    ab_im = mag * jnp.sin(ang)
    den = lr * lr + li * li
    f_re = ((ab_re - 1.0) * lr + ab_im * li) / den
    f_im = (ab_im * lr - (ab_re - 1.0) * li) / den
    br = b_re.astype(F32)
    bi = b_im.astype(F32)
    bb_re = f_re[..., None] * br - f_im[..., None] * bi
    bb_im = f_re[..., None] * bi + f_im[..., None] * br
    g = lr.shape[0]
    gpc = S5_GROUPS_PER_CHUNK
    n_chunks = g // gpc
    eye = jnp.eye(gpc, dtype=F32)

    def b_block(bb):
        t = bb.reshape(n_chunks, gpc, SSM_STATE, SSM_GROUP).transpose(0, 1, 3, 2)
        blk = t[:, :, :, None, :] * eye[None, :, None, :, None]
        return blk.reshape(n_chunks, gpc * SSM_GROUP, gpc * SSM_STATE)

    def c_block(cc):
        t = cc.reshape(n_chunks, gpc, SSM_GROUP, SSM_STATE).transpose(0, 1, 3, 2)
        blk = t[:, :, :, None, :] * eye[None, :, None, :, None]
        return blk.reshape(n_chunks, gpc * SSM_STATE, gpc * SSM_GROUP)

    bblk = jnp.concatenate([b_block(bb_re), b_block(bb_im)], axis=2).astype(BF16)
    cblk = jnp.concatenate([c_block(c_re.astype(F32)), -c_block(c_im.astype(F32))], axis=1).astype(BF16)
    a_re = ab_re.reshape(n_chunks, 1, gpc * SSM_STATE)
    a_im = ab_im.reshape(n_chunks, 1, gpc * SSM_STATE)
    return bblk, cblk, a_re, a_im


def _rope_tables(pos, n_rep):
    inv = 1.0 / (ROPE_THETA ** (jnp.arange(0, QK_ROPE, 2, dtype=F32) / QK_ROPE))
    ang = pos.astype(F32)[:, None] * inv[None, :]
    cos = jnp.cos(ang)
    sin = jnp.sin(ang)
    length = pos.shape[0]
    zero_lo = jnp.zeros((length, QK_NOPE), F32)
    zero_hi = jnp.zeros((length, HEAD_PAD - QK_NOPE - QK_ROPE), F32)
    cos_k = jnp.concatenate([zero_lo, cos, cos, zero_hi], axis=1)
    sin_k = jnp.concatenate([zero_lo, sin, sin, zero_hi], axis=1)
    cos_q = jnp.concatenate([jnp.ones((length, QK_NOPE), F32), cos, cos, zero_hi], axis=1)
    if n_rep > 1:
        cos_k, sin_k, cos_q = (jnp.tile(a, (n_rep, 1)) for a in (cos_k, sin_k, cos_q))
    return cos_k, sin_k, cos_q


def _pad_heads(w, n_heads, width, offset=0):
    k = w.shape[0]
    w3 = w.reshape(k, n_heads, width)
    out = jnp.zeros((k, n_heads, HEAD_PAD), w.dtype).at[:, :, offset:offset + width].set(w3)
    return out.reshape(k, n_heads * HEAD_PAD)


def _swap_rope(w_rope):
    half = QK_ROPE // 2
    return jnp.concatenate([-w_rope[..., half:], w_rope[..., :half]], axis=-1)


def _mla_weights(p, b):
    n_heads = p["mla_w_uk"].shape[1]
    kvl = p["mla_w_uk"].shape[0]
    w_qb = p["mla_w_q_b"][b]
    ql = w_qb.shape[0]
    w3 = w_qb.reshape(ql, n_heads, QK_NOPE + QK_ROPE)
    zero_tail = jnp.zeros((ql, n_heads, HEAD_PAD - QK_NOPE - QK_ROPE), F32)
    wqb_p = jnp.concatenate([w3, zero_tail], axis=2).reshape(ql, n_heads * HEAD_PAD)
    wqb_s = jnp.concatenate([jnp.zeros((ql, n_heads, QK_NOPE), F32), _swap_rope(w3[..., QK_NOPE:]),
                             zero_tail], axis=2).reshape(ql, n_heads * HEAD_PAD)
    w_o = p["mla_w_o"][b]
    d_model = w_o.shape[1]
    w_o_p = jnp.zeros((n_heads, HEAD_PAD, d_model), F32).at[:, :V_HEAD].set(
        w_o.reshape(n_heads, V_HEAD, d_model)).reshape(n_heads * HEAD_PAD, d_model)
    del kvl
    return dict(wqa=p["mla_w_q_a"][b].astype(BF16), g_q=p["mla_g_q"][b][None],
                wqb_p=wqb_p.astype(BF16), wqb_s=wqb_s.astype(BF16),
                w_o=w_o.astype(BF16), w_o_p=w_o_p.astype(BF16))


def _shared_mla_weights(p):
    w_kv_a = p["mla_w_kv_a"]
    d_model = w_kv_a.shape[0]
    kvl, n_heads, _ = p["mla_w_uk"].shape
    w_rope = w_kv_a[:, kvl:]
    pad_lo = jnp.zeros((d_model, QK_NOPE), F32)
    pad_hi = jnp.zeros((d_model, HEAD_PAD - QK_NOPE - QK_ROPE), F32)
    wr = jnp.concatenate([pad_lo, w_rope, pad_hi], axis=1)
    wrs = jnp.concatenate([pad_lo, _swap_rope(w_rope), pad_hi], axis=1)
    wuk_p = _pad_heads(p["mla_w_uk"].reshape(kvl, n_heads * QK_NOPE), n_heads, QK_NOPE)
    wuv_p = _pad_heads(p["mla_w_uv"].reshape(kvl, n_heads * V_HEAD), n_heads, V_HEAD)
    width = kvl + LANES
    m_heads = jnp.zeros((n_heads, HEAD_PAD, width), F32)
    m_heads = m_heads.at[:, :QK_NOPE, :kvl].set(p["mla_w_uk"].transpose(1, 2, 0))
    m_heads = m_heads.at[:, QK_NOPE:QK_NOPE + QK_ROPE, kvl:kvl + QK_ROPE].set(
        jnp.broadcast_to(jnp.eye(QK_ROPE, dtype=F32), (n_heads, QK_ROPE, QK_ROPE)))
    return dict(wc=w_kv_a[:, :kvl].astype(BF16), g_kv=p["mla_g_kv"][None], wr=wr.astype(BF16),
                wrs=wrs.astype(BF16), wuk_p=wuk_p.astype(BF16), wuv_p=wuv_p.astype(BF16),
                m_heads=m_heads.astype(BF16), w_uv_hm=p["mla_w_uv"].transpose(1, 0, 2).astype(BF16))


def _tiles(t):
    big = t >= 4096
    return dict(tm=256 if big else min(t, 128), bm=512 if big else 64,
                tt_route=512 if big else min(t, 128), tt_scatter=256 if big else min(t, 128),
                tt_combine=128 if big else min(t, 128))


def _trunk(x, pos, h0, past, p, s5w, mla_shared, mla_layers):
    n_seq, seq, d_model = x.shape
    t = n_seq * seq
    cfg = _tiles(t)
    n_a = len(s5w)
    n_states = d_model // SSM_GROUP * SSM_STATE

    assert d_model == ROW_TILE * LANES, "MoE row buffers hold one (8, 128) tile per row"

    def moe(proj_out, layer):
        x1, x1_rows, logits = proj_out
        return _moe_layer(x1, x1_rows, logits, p, layer, bm=cfg["bm"], tt_route=cfg["tt_route"],
                          tt_scatter=cfg["tt_scatter"], tt_combine=cfg["tt_combine"])

    xt = x.transpose(1, 0, 2).reshape(t, d_model)
    lt = min(seq, 256)
    new_re, new_im = [], []
    for a in range(n_a):
        bblk, cblk, a_re, a_im = s5w[a]
        if h0 is None:
            h0r = jnp.zeros((n_seq, n_states), F32)
            h0i = jnp.zeros((n_seq, n_states), F32)
        else:
            h0r = h0[0][a].reshape(n_seq, n_states)
            h0i = h0[1][a].reshape(n_seq, n_states)
        z, hr, hi = _s5_scan(xt, bblk, cblk, a_re, a_im, p["ssm_d"][a][None], h0r, h0i,
                             n_seq=n_seq, lt=lt)
        new_re.append(hr.reshape(n_seq, d_model // SSM_GROUP, SSM_STATE))
        new_im.append(hi.reshape(n_seq, d_model // SSM_GROUP, SSM_STATE))
        proj = _proj_ln_router(z, p["ssm_w_glu_bf"][a], p["ssm_b_glu"][a][None], xt,
                               p["ln_mix_g"][a][None], p["ln_mix_b"][a][None],
                               p["moe_w_router"][a], p["moe_b_router"][a][None],
                               glu=True, tm=cfg["tm"])
        xt = moe(proj, a)

    xb = xt.reshape(seq, n_seq, d_model).transpose(1, 0, 2).reshape(t, d_model)
    tm = cfg["tm"]
    n_rep = 1 if seq >= tm else tm // seq
    cos_k, sin_k, cos_q = _rope_tables(pos, n_rep)
    prompt = past is None
    lat = _mla_latent(xb, mla_shared["wc"], mla_shared["g_kv"], mla_shared["wr"], mla_shared["wrs"],
                      cos_k, sin_k, mla_shared["wuk_p"], mla_shared["wuv_p"], tm=tm, with_kv=prompt)
    c_lat, kr128 = lat[0], lat[1]
    kr = kr128[:, QK_NOPE:QK_NOPE + QK_ROPE]
    kvl = c_lat.shape[1]
    n_heads = mla_shared["m_heads"].shape[0]
    zero_bias = jnp.zeros((1, d_model), F32)
    if not prompt:
        cache_ckv, cache_krope, page_table = past
        n_new = -(-seq // SUBLANES) * SUBLANES
        c_new = jnp.zeros((n_seq, n_new, kvl), F32).at[:, :seq].set(c_lat.reshape(n_seq, seq, kvl))
        kr_new = jnp.zeros((n_seq, n_new, QK_ROPE), F32).at[:, :seq].set(kr.reshape(n_seq, seq, QK_ROPE))
    for b, mw in enumerate(mla_layers):
        layer = n_a + b
        qp = _mla_queries(xb, mw["wqa"], mw["g_q"], mw["wqb_p"], mw["wqb_s"], cos_q, sin_k, tm=tm)
        if prompt:
            o = _flash_attention(qp, lat[2], lat[3], n_seq=n_seq, tq=min(seq, 512))
            w_o = mw["w_o_p"]
        else:
            qcat = _q_latent(qp, mla_shared["m_heads"])
            width = qcat.shape[2]
            qcat = qcat.reshape(n_heads, n_seq, seq, width).transpose(1, 0, 2, 3).reshape(
                n_seq * n_heads * seq, width)
            o_lat = _paged_attention(qcat, c_new, kr_new, cache_ckv, cache_krope, page_table,
                                     dec_seq=seq, n_heads=n_heads,
                                     pages_per_step=math.gcd(page_table.shape[1], PAGES_PER_STEP))
            o_hm = o_lat.reshape(n_seq, n_heads, seq, kvl).transpose(1, 0, 2, 3).reshape(n_heads, t, kvl)
            o = _value_up(o_hm, mla_shared["w_uv_hm"])
            o = o.transpose(1, 0, 2).reshape(t, n_heads * V_HEAD)
            w_o = mw["w_o"]
        proj = _proj_ln_router(o, w_o, zero_bias, xb, p["ln_mix_g"][layer][None],
                               p["ln_mix_b"][layer][None], p["moe_w_router"][layer],
                               p["moe_b_router"][layer][None], glu=False, tm=tm)
        xb = moe(proj, layer)
    y = xb.reshape(n_seq, seq, d_model)
    return (y, jnp.stack(new_re), jnp.stack(new_im), c_lat.reshape(n_seq, seq, kvl),
            kr.reshape(n_seq, seq, QK_ROPE))


def kernel(x_prompt, x_sample, state_ssm_re, state_ssm_im, cache_ckv, cache_krope, page_table,
           ssm_lam_re, ssm_lam_im, ssm_log_dt, ssm_b_re, ssm_b_im, ssm_c_re, ssm_c_im, ssm_d,
           ssm_w_glu, ssm_b_glu, mla_w_kv_a, mla_g_kv, mla_w_uk, mla_w_uv, mla_w_q_a, mla_g_q,
           mla_w_q_b, mla_w_o, moe_w_router, moe_b_router, moe_w_gu, moe_b_gu, moe_w_dn, moe_b_dn,
           ln_mix_g, ln_mix_b, ln_ffn_g, ln_ffn_b):
    p = dict(ssm_d=ssm_d, ssm_b_glu=ssm_b_glu, ssm_w_glu_bf=ssm_w_glu.astype(BF16),
             mla_w_kv_a=mla_w_kv_a, mla_g_kv=mla_g_kv, mla_w_uk=mla_w_uk, mla_w_uv=mla_w_uv,
             mla_w_q_a=mla_w_q_a, mla_g_q=mla_g_q, mla_w_q_b=mla_w_q_b, mla_w_o=mla_w_o,
             moe_w_router=moe_w_router, moe_b_router=moe_b_router, moe_w_gu=moe_w_gu,
             moe_b_gu4=moe_b_gu[:, :, None, :], moe_w_dn=moe_w_dn, moe_b_dn4=moe_b_dn[:, :, None, :],
             ln_mix_g=ln_mix_g, ln_mix_b=ln_mix_b, ln_ffn_g=ln_ffn_g, ln_ffn_b=ln_ffn_b)
    n_a = ssm_lam_re.shape[0]
    s5w = [_s5_weights(ssm_lam_re[a], ssm_lam_im[a], ssm_log_dt[a], ssm_b_re[a], ssm_b_im[a],
                       ssm_c_re[a], ssm_c_im[a]) for a in range(n_a)]
    mla_shared = _shared_mla_weights(p)
    mla_layers = [_mla_weights(p, b) for b in range(mla_w_q_a.shape[0])]

    n_pages = page_table.shape[1]
    past_len = n_pages * cache_ckv.shape[1]
    pos_sample = past_len + jnp.arange(x_sample.shape[1])
    out_s = _trunk(x_sample, pos_sample, (state_ssm_re, state_ssm_im),
                   (cache_ckv, cache_krope, page_table), p, s5w, mla_shared, mla_layers)
    pos_prompt = jnp.arange(x_prompt.shape[1])
    out_p = _trunk(x_prompt, pos_prompt, None, None, p, s5w, mla_shared, mla_layers)
    return (out_p[0], out_s[0], out_p[1], out_p[2], out_p[3], out_p[4],
            out_s[1], out_s[2], out_s[3], out_s[4])
```

```python
import functools
import math

import jax
import jax.numpy as jnp
from jax import lax
from jax.experimental import pallas as pl
from jax.experimental.pallas import tpu as pltpu

F32 = jnp.float32
BF16 = jnp.bfloat16

SSM_GROUP = 16
SSM_STATE = 64
QK_NOPE = 64
QK_ROPE = 32
V_HEAD = 64
ROPE_THETA = 10000.0
ATTN_SCALE = (QK_NOPE + QK_ROPE) ** -0.5
SCALE_LOG2E = ATTN_SCALE * math.log2(math.e)
TOP_K = 4
SWIGLU_LIMIT = 7.0
SWIGLU_ALPHA = 1.702
LN_EPS = 1e-5
RMS_EPS = 1e-6
DEPTH = 4
DEEPNORM_ALPHA = (2 * DEPTH) ** 0.25

LANES = 128
SUBLANES = 8
ROW_TILE = SUBLANES
HEAD_PAD = LANES
PAGES_PER_STEP = 64
DMA_UNROLL = 8
VMEM_LIMIT = 56 * 1024 * 1024

S5_CHUNK = LANES
S5_GROUPS_PER_CHUNK = S5_CHUNK // SSM_GROUP
S5_STATES_PER_CHUNK = S5_GROUPS_PER_CHUNK * SSM_STATE


def _params(sem, vmem=VMEM_LIMIT):
    return pltpu.CompilerParams(dimension_semantics=sem, vmem_limit_bytes=vmem)


def _layer_norm(x, g, b):
    mu = jnp.mean(x, axis=-1, keepdims=True)
    xc = x - mu
    var = jnp.mean(xc * xc, axis=-1, keepdims=True)
    return xc * lax.rsqrt(var + LN_EPS) * g + b


def _rms_norm(x, g):
    return x * lax.rsqrt(jnp.mean(x * x, axis=-1, keepdims=True) + RMS_EPS) * g


def _gelu_tanh(x):
    c = math.sqrt(2.0 / math.pi)
    return 0.5 * x * (1.0 + jnp.tanh(c * (x + 0.044715 * (x * x * x))))


def _s5_kernel(x_ref, bblk_ref, cblk_ref, are_ref, aim_ref, d_ref, h0r_ref, h0i_ref,
               z_ref, hr_out, hi_out, hbuf, st_re, st_im, *, n_seq, lt):
    i = pl.program_id(1)
    ns = S5_STATES_PER_CHUNK

    @pl.when(i == 0)
    def _():
        st_re[...] = h0r_ref[...]
        st_im[...] = h0i_ref[...]

    u = x_ref[...]
    hbuf[...] = jnp.dot(u.astype(BF16), bblk_ref[...], preferred_element_type=F32)
    a_re = jnp.broadcast_to(are_ref[...], (n_seq, ns))
    a_im = jnp.broadcast_to(aim_ref[...], (n_seq, ns))

    def step(l, carry):
        h_re, h_im = carry
        r0 = pl.multiple_of(l * n_seq, n_seq)
        bu_re = hbuf[pl.ds(r0, n_seq), 0:ns]
        bu_im = hbuf[pl.ds(r0, n_seq), ns:2 * ns]
        n_re = a_re * h_re - a_im * h_im + bu_re
        n_im = a_re * h_im + a_im * h_re + bu_im
        hbuf[pl.ds(r0, n_seq), 0:ns] = n_re
        hbuf[pl.ds(r0, n_seq), ns:2 * ns] = n_im
        return n_re, n_im

    h_re, h_im = lax.fori_loop(0, lt, step, (st_re[...], st_im[...]), unroll=min(lt, 8))
    st_re[...] = h_re
    st_im[...] = h_im

    y = jnp.dot(hbuf[...].astype(BF16), cblk_ref[...], preferred_element_type=F32)
    z_ref[...] = _gelu_tanh(y + d_ref[...] * u)

    @pl.when(i == pl.num_programs(1) - 1)
    def _():
        hr_out[...] = h_re
        hi_out[...] = h_im


def _s5_scan(x_tm, bblk, cblk, a_re, a_im, d_skip, h0_re, h0_im, *, n_seq, lt):
    rows, d_model = x_tm.shape
    n_chunks = d_model // S5_CHUNK
    ns = S5_STATES_PER_CHUNK
    seq = rows // n_seq
    tile = lt * n_seq
    kern = functools.partial(_s5_kernel, n_seq=n_seq, lt=lt)
    return pl.pallas_call(
        kern,
        grid=(n_chunks, seq // lt),
        in_specs=[
            pl.BlockSpec((tile, S5_CHUNK), lambda c, i: (i, c)),
            pl.BlockSpec((None, S5_CHUNK, 2 * ns), lambda c, i: (c, 0, 0)),
            pl.BlockSpec((None, 2 * ns, S5_CHUNK), lambda c, i: (c, 0, 0)),
            pl.BlockSpec((None, 1, ns), lambda c, i: (c, 0, 0)),
            pl.BlockSpec((None, 1, ns), lambda c, i: (c, 0, 0)),
            pl.BlockSpec((1, S5_CHUNK), lambda c, i: (0, c)),
            pl.BlockSpec((n_seq, ns), lambda c, i: (0, c)),
            pl.BlockSpec((n_seq, ns), lambda c, i: (0, c)),
        ],
        out_specs=[
            pl.BlockSpec((tile, S5_CHUNK), lambda c, i: (i, c)),
            pl.BlockSpec((n_seq, ns), lambda c, i: (0, c)),
            pl.BlockSpec((n_seq, ns), lambda c, i: (0, c)),
        ],
        out_shape=[
            jax.ShapeDtypeStruct((rows, d_model), F32),
            jax.ShapeDtypeStruct((n_seq, n_chunks * ns), F32),
            jax.ShapeDtypeStruct((n_seq, n_chunks * ns), F32),
        ],
        scratch_shapes=[
            pltpu.VMEM((tile, 2 * ns), F32),
            pltpu.VMEM((n_seq, ns), F32),
            pltpu.VMEM((n_seq, ns), F32),
        ],
        compiler_params=_params(("arbitrary", "arbitrary")),
        name="s5_scan",
    )(x_tm, bblk, cblk, a_re, a_im, d_skip, h0_re, h0_im)


def _split_bf16(v):
    hi = v.astype(BF16)
    return hi, (v - hi.astype(F32)).astype(BF16)


def _proj_ln_router_kernel(a_ref, w_ref, b_ref, x_ref, g_ref, beta_ref, wrh_ref, wrl_ref, br_ref,
                           x1_ref, xrow_ref, lg_ref, *, glu):
    h = jnp.dot(a_ref[...].astype(BF16), w_ref[...], preferred_element_type=F32) + b_ref[...]
    if glu:
        d = h.shape[1] // 2
        mix = h[:, :d] * jax.nn.sigmoid(h[:, d:])
    else:
        mix = h
    x1 = _layer_norm(DEEPNORM_ALPHA * x_ref[...] + mix, g_ref[...], beta_ref[...])
    x1_ref[...] = x1
    tm, d_model = x1.shape
    for c in range(d_model // LANES):
        xrow_ref[pl.ds(c, tm, stride=ROW_TILE), :] = x1[:, c * LANES:(c + 1) * LANES]
    xh, xl = _split_bf16(x1)
    lg_ref[...] = (jnp.dot(xh, wrh_ref[...], preferred_element_type=F32)
                   + jnp.dot(xl, wrh_ref[...], preferred_element_type=F32)
                   + jnp.dot(xh, wrl_ref[...], preferred_element_type=F32)) + br_ref[...]


def _proj_ln_router(a, w, b, x, g, beta, wr, br, *, glu, tm):
    t, k = a.shape
    nw = w.shape[1]
    d = x.shape[1]
    ne = wr.shape[1]
    wrh, wrl = _split_bf16(wr.astype(F32))
    kern = functools.partial(_proj_ln_router_kernel, glu=glu)
    return pl.pallas_call(
        kern,
        grid=(t // tm,),
        in_specs=[
            pl.BlockSpec((tm, k), lambda i: (i, 0)),
            pl.BlockSpec((k, nw), lambda i: (0, 0)),
            pl.BlockSpec((1, nw), lambda i: (0, 0)),
            pl.BlockSpec((tm, d), lambda i: (i, 0)),
            pl.BlockSpec((1, d), lambda i: (0, 0)),
            pl.BlockSpec((1, d), lambda i: (0, 0)),
            pl.BlockSpec((d, ne), lambda i: (0, 0)),
            pl.BlockSpec((d, ne), lambda i: (0, 0)),
            pl.BlockSpec((1, ne), lambda i: (0, 0)),
        ],
        out_specs=[
            pl.BlockSpec((tm, d), lambda i: (i, 0)),
            pl.BlockSpec((tm * ROW_TILE, LANES), lambda i: (i, 0)),
            pl.BlockSpec((tm, ne), lambda i: (i, 0)),
        ],
        out_shape=[
            jax.ShapeDtypeStruct((t, d), F32),
            jax.ShapeDtypeStruct((t * ROW_TILE, LANES), F32),
            jax.ShapeDtypeStruct((t, ne), F32),
        ],
        compiler_params=_params(("arbitrary",)),
        name="proj_ln_router",
    )(a, w, b, x, g, beta, wrh, wrl, br)


def _route_kernel(lg_ref, tri_ref, eidx_ref, rank_ref, gate_ref, cnt_ref, base_scr):
    i = pl.program_id(0)

    @pl.when(i == 0)
    def _():
        base_scr[...] = jnp.zeros_like(base_scr)

    l = lg_ref[...]
    tt, ne = l.shape
    lane = lax.broadcasted_iota(jnp.int32, (tt, ne), 1).astype(F32)
    tops, idxs, hots = [], [], []
    for _ in range(TOP_K):
        m = jnp.max(l, axis=1, keepdims=True)
        idx = jnp.min(jnp.where(l == m, lane, float(ne)), axis=1, keepdims=True)
        hot = lane == idx
        tops.append(m)
        idxs.append(idx.astype(jnp.int32))
        hots.append(hot)
        l = jnp.where(hot, -jnp.inf, l)
    sel = jnp.zeros((tt, ne), F32)
    for hot in hots:
        sel = sel + jnp.where(hot, 1.0, 0.0)
    prefix = jnp.dot(tri_ref[...], sel.astype(BF16), preferred_element_type=F32) + base_scr[...]
    es = [jnp.exp(m - tops[0]) for m in tops]
    den = es[0] + es[1] + es[2] + es[3]
    wide = lax.broadcasted_iota(jnp.int32, (tt, LANES), 1)
    eidx_w = jnp.zeros((tt, LANES), jnp.int32)
    rank_w = jnp.zeros((tt, LANES), jnp.int32)
    gate_w = jnp.zeros((tt, LANES), F32)
    for k in range(TOP_K):
        rank_k = jnp.sum(jnp.where(hots[k], prefix, 0.0), axis=1, keepdims=True).astype(jnp.int32)
        eidx_w = jnp.where(wide == k, idxs[k], eidx_w)
        rank_w = jnp.where(wide == k, rank_k, rank_w)
        gate_w = jnp.where(wide == k, es[k] / den, gate_w)
    eidx_ref[...] = eidx_w
    rank_ref[...] = rank_w
    gate_ref[...] = gate_w
    base_scr[...] = base_scr[...] + jnp.sum(sel, axis=0, keepdims=True)
    cnt_ref[...] = base_scr[...].astype(jnp.int32)


def _route(logits, *, tt):
    t, ne = logits.shape
    tri = jnp.tri(tt, k=-1, dtype=BF16)
    return pl.pallas_call(
        _route_kernel,
        grid=(t // tt,),
        in_specs=[
            pl.BlockSpec((tt, ne), lambda i: (i, 0)),
            pl.BlockSpec((tt, tt), lambda i: (0, 0)),
        ],
        out_specs=[
            pl.BlockSpec((tt, LANES), lambda i: (i, 0)),
            pl.BlockSpec((tt, LANES), lambda i: (i, 0)),
            pl.BlockSpec((tt, LANES), lambda i: (i, 0)),
            pl.BlockSpec((1, ne), lambda i: (0, 0)),
        ],
        out_shape=[
            jax.ShapeDtypeStruct((t, LANES), jnp.int32),
            jax.ShapeDtypeStruct((t, LANES), jnp.int32),
            jax.ShapeDtypeStruct((t, LANES), F32),
            jax.ShapeDtypeStruct((1, ne), jnp.int32),
        ],
        scratch_shapes=[pltpu.VMEM((1, ne), F32)],
        compiler_params=_params(("arbitrary",)),
        name="moe_route",
    )(logits, tri)


def _tile_rows(row):
    return pl.ds(pl.multiple_of(row * ROW_TILE, ROW_TILE), ROW_TILE)


def _scatter_kernel(dest_ref, pad_ref, x_ref, rows_out, zero_row, sem, zsem):
    i = pl.program_id(0)
    n_assign = dest_ref.shape[1]
    tt = n_assign // TOP_K

    def row_copy(src_row, dst_row):
        return pltpu.make_async_copy(x_ref.at[_tile_rows(src_row)], rows_out.at[_tile_rows(dst_row)], sem)

    def drain(j, carry):
        row_copy(0, 0).wait()
        return carry

    for k in range(TOP_K):
        def issue(g, carry, k=k):
            for u in range(DMA_UNROLL):
                t = g * DMA_UNROLL + u
                row_copy(t, dest_ref[0, k * tt + t]).start(priority=u % 2)
            return carry
        lax.fori_loop(0, tt // DMA_UNROLL, issue, 0)

    @pl.when(i == pl.num_programs(0) - 1)
    def _():
        zero_row[...] = jnp.zeros_like(zero_row)

        def zero_copy(dst_row):
            return pltpu.make_async_copy(zero_row, rows_out.at[_tile_rows(dst_row)], zsem)

        def per_expert(e, n_pad):
            lo = pad_ref[0, e]
            hi = pad_ref[1, e]

            def body(r, carry):
                zero_copy(r).start()
                return carry

            lax.fori_loop(lo, hi, body, 0)
            return n_pad + (hi - lo)

        n_pad = lax.fori_loop(0, pad_ref.shape[1], per_expert, 0)

        def zdrain(r, carry):
            zero_copy(0).wait()
            return carry

        lax.fori_loop(0, n_pad, zdrain, 0)

    lax.fori_loop(0, n_assign, drain, 0, unroll=DMA_UNROLL)


def _k_major_tiles(dest, tt):
    t = dest.shape[0]
    return dest.reshape(t // tt, tt, TOP_K).transpose(0, 2, 1).reshape(t // tt, 1, TOP_K * tt)


def _moe_scatter(x_rows_src, dest, pad_ranges, n_rows, *, tt):
    t = x_rows_src.shape[0] // ROW_TILE
    dest2 = _k_major_tiles(dest, tt)
    return pl.pallas_call(
        _scatter_kernel,
        grid=(t // tt,),
        in_specs=[
            pl.BlockSpec((None, 1, tt * TOP_K), lambda i: (i, 0, 0), memory_space=pltpu.SMEM),
            pl.BlockSpec(pad_ranges.shape, lambda i: (0, 0), memory_space=pltpu.SMEM),
            pl.BlockSpec((tt * ROW_TILE, LANES), lambda i: (i, 0)),
        ],
        out_specs=pl.BlockSpec(memory_space=pl.ANY),
        out_shape=jax.ShapeDtypeStruct((n_rows * ROW_TILE, LANES), F32),
        scratch_shapes=[pltpu.VMEM((ROW_TILE, LANES), F32), pltpu.SemaphoreType.DMA(()),
                        pltpu.SemaphoreType.DMA(())],
        compiler_params=_params(("arbitrary",)),
        name="moe_scatter",
    )(dest2, pad_ranges, x_rows_src)


def _ffn_kernel(be_ref, nu_ref, x_ref, xs_ref, wgu_ref, bgu_ref, wdn_ref, bdn_ref, y_ref, ys_ref,
                wgu_bf, wdn_bf, x_bf):
    b = pl.program_id(0)
    prev = be_ref[jnp.maximum(b - 1, 0)]
    changed = jnp.logical_or(b == 0, be_ref[b] != prev)
    bm, d_model = x_bf.shape
    n_lane_tiles = d_model // LANES

    def expert_ffn(xb):
        de = wdn_bf.shape[0]
        h = jnp.dot(xb, wgu_bf[...], preferred_element_type=F32) + bgu_ref[...]
        g = jnp.minimum(h[:, :de], SWIGLU_LIMIT)
        up = jnp.clip(h[:, de:], -SWIGLU_LIMIT, SWIGLU_LIMIT)
        act = (up + 1.0) * (g * jax.nn.sigmoid(SWIGLU_ALPHA * g))
        return jnp.dot(act.astype(BF16), wdn_bf[...], preferred_element_type=F32) + bdn_ref[...]

    @pl.when(changed)
    def _():
        wgu_bf[...] = wgu_ref[...].astype(BF16)
        wdn_bf[...] = wdn_ref[...].astype(BF16)
        ys_ref[...] = expert_ffn(xs_ref[...])

    @pl.when(b < nu_ref[0])
    def _():
        for c in range(n_lane_tiles):
            x_bf[:, c * LANES:(c + 1) * LANES] = x_ref[pl.ds(c, bm, stride=ROW_TILE), :].astype(BF16)
        y = expert_ffn(x_bf[...])
        for c in range(n_lane_tiles):
            y_ref[pl.ds(c, bm, stride=ROW_TILE), :] = y[:, c * LANES:(c + 1) * LANES]

    @pl.when(b >= nu_ref[0])
    def _():
        y_ref[...] = jnp.zeros_like(y_ref)


def _moe_ffn(x_rows, xs_bf, block_e, n_used, w_gu, b_gu, w_dn, b_dn, layer, *, bm):
    n_rows = x_rows.shape[0] // ROW_TILE
    ts = xs_bf.shape[0]
    ne = w_dn.shape[1]
    d, de = w_dn.shape[3], w_dn.shape[2]
    grid_spec = pltpu.PrefetchScalarGridSpec(
        num_scalar_prefetch=2,
        grid=(n_rows // bm,),
        in_specs=[
            pl.BlockSpec((bm * ROW_TILE, LANES), lambda b, be, nu: (jnp.minimum(b, nu[0] - 1), 0)),
            pl.BlockSpec((ts, d), lambda b, be, nu: (0, 0)),
            pl.BlockSpec((None, None, d, 2 * de), lambda b, be, nu: (layer, be[b], 0, 0)),
            pl.BlockSpec((None, None, 1, 2 * de), lambda b, be, nu: (layer, be[b], 0, 0)),
            pl.BlockSpec((None, None, de, d), lambda b, be, nu: (layer, be[b], 0, 0)),
            pl.BlockSpec((None, None, 1, d), lambda b, be, nu: (layer, be[b], 0, 0)),
        ],
        out_specs=[
            pl.BlockSpec((bm * ROW_TILE, LANES), lambda b, be, nu: (b, 0)),
            pl.BlockSpec((None, ts, d), lambda b, be, nu: (be[b], 0, 0)),
        ],
        scratch_shapes=[pltpu.VMEM((d, 2 * de), BF16), pltpu.VMEM((de, d), BF16),
                        pltpu.VMEM((bm, d), BF16)],
    )
    return pl.pallas_call(
        _ffn_kernel,
        grid_spec=grid_spec,
        out_shape=[jax.ShapeDtypeStruct((n_rows * ROW_TILE, LANES), F32),
                   jax.ShapeDtypeStruct((ne, ts, d), F32)],
        compiler_params=_params(("arbitrary",)),
        name="moe_ffn",
    )(block_e, n_used, x_rows, xs_bf, w_gu, b_gu, w_dn, b_dn)


def _dense_combine_kernel(gate_ref, ys_ref, x_ref, g_ref, beta_ref, out_ref, acc):
    e = pl.program_id(0)

    @pl.when(e == 0)
    def _():
        acc[...] = jnp.zeros_like(acc)

    gate = gate_ref[...]
    acc[...] += jnp.where(gate != 0.0, gate * ys_ref[...], 0.0)

    @pl.when(e == pl.num_programs(0) - 1)
    def _():
        out_ref[...] = _layer_norm(DEEPNORM_ALPHA * x_ref[...] + acc[...], g_ref[...], beta_ref[...])


def _moe_dense_combine(ys, gate_dense, x1, g, beta):
    ne, ts, d = ys.shape
    return pl.pallas_call(
        _dense_combine_kernel,
        grid=(ne,),
        in_specs=[
            pl.BlockSpec((None, ts, 1), lambda e: (e, 0, 0)),
            pl.BlockSpec((None, ts, d), lambda e: (e, 0, 0)),
            pl.BlockSpec((ts, d), lambda e: (0, 0)),
            pl.BlockSpec((1, d), lambda e: (0, 0)),
            pl.BlockSpec((1, d), lambda e: (0, 0)),
        ],
        out_specs=pl.BlockSpec((ts, d), lambda e: (0, 0)),
        out_shape=jax.ShapeDtypeStruct((ts, d), F32),
        scratch_shapes=[pltpu.VMEM((ts, d), F32)],
        compiler_params=_params(("arbitrary",)),
        name="moe_dense_combine",
    )(gate_dense, ys, x1, g, beta)


def _combine_kernel(dcur_ref, dnxt_ref, gate_ref, x_ref, g_ref, beta_ref, yrows, out_ref, buf, sem):
    i = pl.program_id(0)
    n_steps = pl.num_programs(0)
    n_assign = dcur_ref.shape[1]
    tt = n_assign // TOP_K

    def row_copy(dref, k, t, slot):
        return pltpu.make_async_copy(yrows.at[_tile_rows(dref[0, k * tt + t])],
                                     buf.at[slot, k, _tile_rows(t)], sem.at[slot])

    def issue(dref, slot):
        for k in range(TOP_K):
            def body(g, carry, k=k):
                for u in range(DMA_UNROLL):
                    row_copy(dref, k, g * DMA_UNROLL + u, slot).start(priority=u % 2)
                return carry
            lax.fori_loop(0, tt // DMA_UNROLL, body, 0)

    @pl.when(i == 0)
    def _():
        issue(dcur_ref, 0)

    @pl.when(i + 1 < n_steps)
    def _():
        issue(dnxt_ref, (i + 1) % 2)

    slot = i % 2

    def drain(j, carry):
        row_copy(dcur_ref, 0, 0, slot).wait()
        return carry

    lax.fori_loop(0, n_assign, drain, 0, unroll=DMA_UNROLL)

    gates = gate_ref[...]
    d_model = x_ref.shape[1]
    pieces = []
    for c in range(d_model // LANES):
        f = gates[:, 0:1] * buf.at[slot, 0][pl.ds(c, tt, stride=ROW_TILE), :]
        for k in range(1, TOP_K):
            f = f + gates[:, k:k + 1] * buf.at[slot, k][pl.ds(c, tt, stride=ROW_TILE), :]
        pieces.append(DEEPNORM_ALPHA * x_ref[:, c * LANES:(c + 1) * LANES] + f)
    out_ref[...] = _layer_norm(jnp.concatenate(pieces, axis=1), g_ref[...], beta_ref[...])


def _moe_combine(y_rows, dest, gate_w, x1, g, beta, *, tt):
    t, d = x1.shape
    n_steps = t // tt
    dest2 = _k_major_tiles(dest, tt)
    return pl.pallas_call(
        _combine_kernel,
        grid=(n_steps,),
        in_specs=[
            pl.BlockSpec((None, 1, tt * TOP_K), lambda i: (i, 0, 0), memory_space=pltpu.SMEM),
            pl.BlockSpec((None, 1, tt * TOP_K), lambda i: (jnp.minimum(i + 1, n_steps - 1), 0, 0),
                         memory_space=pltpu.SMEM),
            pl.BlockSpec((tt, LANES), lambda i: (i, 0)),
            pl.BlockSpec((tt, d), lambda i: (i, 0)),
            pl.BlockSpec((1, d), lambda i: (0, 0)),
            pl.BlockSpec((1, d), lambda i: (0, 0)),
            pl.BlockSpec(memory_space=pl.ANY),
        ],
        out_specs=pl.BlockSpec((tt, d), lambda i: (i, 0)),
        out_shape=jax.ShapeDtypeStruct((t, d), F32),
        scratch_shapes=[pltpu.VMEM((2, TOP_K, tt * ROW_TILE, LANES), F32), pltpu.SemaphoreType.DMA((2,))],
        compiler_params=_params(("arbitrary",)),
        name="moe_combine",
    )(dest2, dest2, gate_w, x1, g, beta, y_rows)


def _moe_pair(big, small, p, layer, *, bm, tt_route, tt_scatter, tt_combine, tt_small):
    x1, x1_rows, logits = big
    xs1, _, logits_s = small
    t, d = x1.shape
    ne = logits.shape[1]
    eidx_w, rank_w, gate_w, cnt = _route(logits, tt=tt_route)
    eidx_s, _, gate_s, _ = _route(logits_s, tt=tt_small)
    experts = jnp.arange(ne, dtype=jnp.int32)
    gate_dense = jnp.sum(jnp.where(eidx_s[:, :TOP_K, None] == experts, gate_s[:, :TOP_K, None], 0.0), axis=1)
    gate_dense = gate_dense.T[:, :, None]
    cnt = cnt[0]
    padded = jnp.maximum((cnt + bm - 1) // bm, 1) * bm
    pend = jnp.cumsum(padded)
    pstart = pend - padded
    onehot_start = jnp.where(eidx_w[:, :TOP_K, None] == experts, pstart, 0)
    dest = (jnp.sum(onehot_start, axis=-1) + rank_w[:, :TOP_K]).astype(jnp.int32)
    n_blocks = (t * TOP_K) // bm + ne
    n_used = pend[-1] // bm
    blk = jnp.arange(n_blocks, dtype=jnp.int32)
    be = jnp.minimum(jnp.sum(pend[None, :] <= blk[:, None] * bm, axis=1), ne - 1).astype(jnp.int32)
    last_e = jnp.sum(jnp.where(blk == n_used - 1, be, 0))
    be = jnp.where(blk < n_used, be, last_e)
    pad_lo = jnp.concatenate([pstart + cnt, pend[-1:]])
    pad_hi = jnp.concatenate([pend, jnp.full((1,), n_blocks * bm, pend.dtype)])
    pad_ranges = jnp.stack([pad_lo, pad_hi]).astype(jnp.int32)
    x_rows = _moe_scatter(x1_rows, dest, pad_ranges, n_blocks * bm, tt=tt_scatter)
    y_rows, ys = _moe_ffn(x_rows, xs1.astype(BF16), be, n_used.reshape(1).astype(jnp.int32),
                          p["moe_w_gu"], p["moe_b_gu4"], p["moe_w_dn"], p["moe_b_dn4"], layer, bm=bm)
    ln_g, ln_b = p["ln_ffn_g"][layer][None], p["ln_ffn_b"][layer][None]
    out_big = _moe_combine(y_rows, dest, gate_w, x1, ln_g, ln_b, tt=tt_combine)
    out_small = _moe_dense_combine(ys, gate_dense, xs1, ln_g, ln_b)
    return out_big, out_small


def _latent_kernel(x_ref, wc_ref, g_ref, wr_ref, wrs_ref, cos_ref, sin_ref, *rest, with_kv):
    if with_kv:
        wuk_ref, wuv_ref, c_ref, kr_ref, kp_ref, vp_ref = rest
    else:
        c_ref, kr_ref = rest
    xb = x_ref[...].astype(BF16)
    kv = jnp.dot(xb, wc_ref[...], preferred_element_type=F32)
    c = _rms_norm(kv, g_ref[...])
    r = jnp.dot(xb, wr_ref[...], preferred_element_type=F32)
    rs = jnp.dot(xb, wrs_ref[...], preferred_element_type=F32)
    kr = r * cos_ref[...] + rs * sin_ref[...]
    c_ref[...] = c
    kr_ref[...] = kr
    if with_kv:
        cb = c.astype(BF16)
        kn = jnp.dot(cb, wuk_ref[...], preferred_element_type=F32)
        n_heads = kn.shape[1] // HEAD_PAD
        for h in range(n_heads):
            sl = slice(h * HEAD_PAD, (h + 1) * HEAD_PAD)
            kp_ref[:, sl] = (kn[:, sl] + kr).astype(BF16)
        vp_ref[...] = jnp.dot(cb, wuv_ref[...], preferred_element_type=F32).astype(BF16)


def _mla_latent(x, wc, g_kv, wr, wrs, cos_k, sin_k, wuk_p, wuv_p, *, tm, with_kv):
    t, d = x.shape
    kvl = wc.shape[1]
    n_tab = cos_k.shape[0] // tm
    in_specs = [
        pl.BlockSpec((tm, d), lambda i: (i, 0)),
        pl.BlockSpec((d, kvl), lambda i: (0, 0)),
        pl.BlockSpec((1, kvl), lambda i: (0, 0)),
        pl.BlockSpec((d, HEAD_PAD), lambda i: (0, 0)),
        pl.BlockSpec((d, HEAD_PAD), lambda i: (0, 0)),
        pl.BlockSpec((tm, HEAD_PAD), lambda i: (i % n_tab, 0)),
        pl.BlockSpec((tm, HEAD_PAD), lambda i: (i % n_tab, 0)),
    ]
    out_specs = [
        pl.BlockSpec((tm, kvl), lambda i: (i, 0)),
        pl.BlockSpec((tm, HEAD_PAD), lambda i: (i, 0)),
    ]
    out_shape = [jax.ShapeDtypeStruct((t, kvl), F32), jax.ShapeDtypeStruct((t, HEAD_PAD), F32)]
    args = [x, wc, g_kv, wr, wrs, cos_k, sin_k]
    if with_kv:
        hp = wuk_p.shape[1]
        in_specs += [pl.BlockSpec((kvl, hp), lambda i: (0, 0)), pl.BlockSpec((kvl, hp), lambda i: (0, 0))]
        out_specs += [pl.BlockSpec((tm, hp), lambda i: (i, 0)), pl.BlockSpec((tm, hp), lambda i: (i, 0))]
        out_shape += [jax.ShapeDtypeStruct((t, hp), BF16), jax.ShapeDtypeStruct((t, hp), BF16)]
        args += [wuk_p, wuv_p]
    return pl.pallas_call(
        functools.partial(_latent_kernel, with_kv=with_kv),
        grid=(t // tm,),
        in_specs=in_specs,
        out_specs=out_specs,
        out_shape=out_shape,
        compiler_params=_params(("arbitrary",)),
        name="mla_latent",
    )(*args)


def _query_kernel(x_ref, wqa_ref, g_ref, wqb_ref, wqs_ref, cos_ref, sin_ref, q_ref):
    xb = x_ref[...].astype(BF16)
    cq = _rms_norm(jnp.dot(xb, wqa_ref[...], preferred_element_type=F32), g_ref[...]).astype(BF16)
    q = jnp.dot(cq, wqb_ref[...], preferred_element_type=F32)
    qs = jnp.dot(cq, wqs_ref[...], preferred_element_type=F32)
    cos = cos_ref[...]
    sin = sin_ref[...]
    n_heads = q.shape[1] // HEAD_PAD
    for h in range(n_heads):
        sl = slice(h * HEAD_PAD, (h + 1) * HEAD_PAD)
        q_ref[:, sl] = (q[:, sl] * cos + qs[:, sl] * sin).astype(BF16)


def _mla_queries(x, wqa, g_q, wqb_p, wqb_s, cos_q, sin_q, *, tm):
    t, d = x.shape
    ql = wqa.shape[1]
    hp = wqb_p.shape[1]
    n_tab = cos_q.shape[0] // tm
    return pl.pallas_call(
        _query_kernel,
        grid=(t // tm,),
        in_specs=[
            pl.BlockSpec((tm, d), lambda i: (i, 0)),
            pl.BlockSpec((d, ql), lambda i: (0, 0)),
            pl.BlockSpec((1, ql), lambda i: (0, 0)),
            pl.BlockSpec((ql, hp), lambda i: (0, 0)),
            pl.BlockSpec((ql, hp), lambda i: (0, 0)),
            pl.BlockSpec((tm, HEAD_PAD), lambda i: (i % n_tab, 0)),
            pl.BlockSpec((tm, HEAD_PAD), lambda i: (i % n_tab, 0)),
        ],
        out_specs=pl.BlockSpec((tm, hp), lambda i: (i, 0)),
        out_shape=jax.ShapeDtypeStruct((t, hp), BF16),
        compiler_params=_params(("arbitrary",)),
        name="mla_queries",
    )(x, wqa, g_q, wqb_p, wqb_s, cos_q, sin_q)


def _flash_kernel(q_ref, k_ref, v_ref, o_ref, *, tq):
    seq = q_ref.shape[0]
    n_tiles = seq // tq
    row = lax.broadcasted_iota(jnp.int32, (tq, tq), 0)
    col = lax.broadcasted_iota(jnp.int32, (tq, tq), 1)
    for i in range(n_tiles):
        q = q_ref[i * tq:(i + 1) * tq, :]
        m = jnp.full((tq, 1), -jnp.inf, F32)
        l = jnp.zeros((tq, 1), F32)
        acc = jnp.zeros((tq, HEAD_PAD), F32)
        for j in range(i + 1):
            k = k_ref[j * tq:(j + 1) * tq, :]
            v = v_ref[j * tq:(j + 1) * tq, :]
            s = lax.dot_general(q, k, (((1,), (1,)), ((), ())), preferred_element_type=F32)
            if j == i:
                s = jnp.where(col <= row, s, -jnp.inf)
            m_new = jnp.maximum(m, jnp.max(s, axis=1, keepdims=True))
            alpha = jnp.exp2((m - m_new) * SCALE_LOG2E)
            pr = jnp.exp2((s - m_new) * SCALE_LOG2E)
            l = alpha * l + jnp.sum(pr, axis=1, keepdims=True)
            acc = alpha * acc + jnp.dot(pr.astype(BF16), v, preferred_element_type=F32)
            m = m_new
        o_ref[i * tq:(i + 1) * tq, :] = (acc / l).astype(o_ref.dtype)


def _flash_attention(qp, kp, vp, *, n_seq, tq):
    t, hp = qp.shape
    seq = t // n_seq
    n_heads = hp // HEAD_PAD
    q3 = qp.reshape(n_seq, seq, hp)
    k3 = kp.reshape(n_seq, seq, hp)
    v3 = vp.reshape(n_seq, seq, hp)
    spec = pl.BlockSpec((None, seq, HEAD_PAD), lambda n, h: (n, 0, h))
    out = pl.pallas_call(
        functools.partial(_flash_kernel, tq=tq),
        grid=(n_seq, n_heads),
        in_specs=[spec, spec, spec],
        out_specs=spec,
        out_shape=jax.ShapeDtypeStruct((n_seq, seq, hp), BF16),
        compiler_params=_params(("arbitrary", "arbitrary")),
        name="flash_attention",
    )(q3, k3, v3)
    return out.reshape(t, hp)


def _qlat_kernel(q_ref, m_ref, o_ref):
    o_ref[...] = jnp.dot(q_ref[...], m_ref[...], preferred_element_type=F32).astype(o_ref.dtype)


def _q_latent(qp, m_heads):
    t, hp = qp.shape
    n_heads, _, width = m_heads.shape
    return pl.pallas_call(
        _qlat_kernel,
        grid=(n_heads,),
        in_specs=[
            pl.BlockSpec((t, HEAD_PAD), lambda h: (0, h)),
            pl.BlockSpec((None, HEAD_PAD, width), lambda h: (h, 0, 0)),
        ],
        out_specs=pl.BlockSpec((None, t, width), lambda h: (h, 0, 0)),
        out_shape=jax.ShapeDtypeStruct((n_heads, t, width), BF16),
        compiler_params=_params(("arbitrary",)),
        name="q_latent",
    )(qp, m_heads)


def _paged_kernel(pt_ref, q_ref, cnew_ref, rnew_ref, ckv_hbm, kr_hbm, o_ref,
                  cbuf, rbuf, sem_c, sem_r, m_scr, l_scr, acc_scr,
                  *, n_chunks, pages_per_step, n_pages, dec_seq, kvl):
    s = pl.program_id(0)
    n_steps = pl.num_programs(0)
    j = s % n_chunks
    page = cbuf.shape[2]

    def copies(step, slot, p):
        pg = pt_ref[(step // n_chunks) * n_pages + (step % n_chunks) * pages_per_step + p]
        return (pltpu.make_async_copy(ckv_hbm.at[pg], cbuf.at[slot, p], sem_c.at[slot]),
                pltpu.make_async_copy(kr_hbm.at[pg], rbuf.at[slot, p], sem_r.at[slot]))

    def issue(step, slot):
        for p in range(pages_per_step):
            cc, cr = copies(step, slot, p)
            cc.start()
            cr.start()

    @pl.when(s == 0)
    def _():
        issue(0, 0)

    @pl.when(s + 1 < n_steps)
    def _():
        issue(s + 1, (s + 1) % 2)

    slot = s % 2
    for p in range(pages_per_step):
        cc, cr = copies(s, slot, p)
        cc.wait()
        cr.wait()

    @pl.when(j == 0)
    def _():
        m_scr[...] = jnp.full_like(m_scr, -jnp.inf)
        l_scr[...] = jnp.zeros_like(l_scr)
        acc_scr[...] = jnp.zeros_like(acc_scr)

    q = q_ref[...]
    q_lat = q[:, :kvl]
    q_pe = q[:, kvl:kvl + QK_ROPE]
    nt = (((1,), (1,)), ((), ()))

    def update(c_b, r_b, mask):
        sc = (lax.dot_general(q_lat, c_b, nt, preferred_element_type=F32)
              + lax.dot_general(q_pe, r_b, nt, preferred_element_type=F32))
        if mask is not None:
            sc = jnp.where(mask, sc, -jnp.inf)
        m_old = m_scr[...]
        m_new = jnp.maximum(m_old, jnp.max(sc, axis=1, keepdims=True))
        alpha = jnp.exp2((m_old - m_new) * SCALE_LOG2E)
        pr = jnp.exp2((sc - m_new) * SCALE_LOG2E)
        l_scr[...] = alpha * l_scr[...] + jnp.sum(pr, axis=1, keepdims=True)
        acc_scr[...] = alpha * acc_scr[...] + jnp.dot(pr.astype(BF16), c_b, preferred_element_type=F32)
        m_scr[...] = m_new

    c_b = cbuf[slot].reshape(pages_per_step * page, kvl).astype(BF16)
    r_b = rbuf[slot].reshape(pages_per_step * page, QK_ROPE).astype(BF16)
    update(c_b, r_b, None)

    @pl.when(j == n_chunks - 1)
    def _():
        rows = q.shape[0]
        n_new = cnew_ref.shape[0]
        q_l = lax.broadcasted_iota(jnp.int32, (rows, n_new), 0) % dec_seq
        kk = lax.broadcasted_iota(jnp.int32, (rows, n_new), 1)
        update(cnew_ref[...].astype(BF16), rnew_ref[...].astype(BF16), kk <= q_l)
        o_ref[...] = acc_scr[...] / l_scr[...]


def _paged_attention(qcat, c_new, kr_new, cache_ckv, cache_krope, page_table, *, dec_seq, n_heads,
                     pages_per_step):
    n_dec, n_pages = page_table.shape
    page, kvl = cache_ckv.shape[1:]
    rows = n_heads * dec_seq
    width = qcat.shape[1]
    n_chunks = n_pages // pages_per_step
    n_new = c_new.shape[1]
    kern = functools.partial(_paged_kernel, n_chunks=n_chunks, pages_per_step=pages_per_step,
                             n_pages=n_pages, dec_seq=dec_seq, kvl=kvl)
    grid_spec = pltpu.PrefetchScalarGridSpec(
        num_scalar_prefetch=1,
        grid=(n_dec * n_chunks,),
        in_specs=[
            pl.BlockSpec((rows, width), lambda s, pt: (s // n_chunks, 0)),
            pl.BlockSpec((None, n_new, kvl), lambda s, pt: (s // n_chunks, 0, 0)),
            pl.BlockSpec((None, n_new, QK_ROPE), lambda s, pt: (s // n_chunks, 0, 0)),
            pl.BlockSpec(memory_space=pl.ANY),
            pl.BlockSpec(memory_space=pl.ANY),
        ],
        out_specs=pl.BlockSpec((rows, kvl), lambda s, pt: (s // n_chunks, 0)),
        scratch_shapes=[
            pltpu.VMEM((2, pages_per_step, page, kvl), F32),
            pltpu.VMEM((2, pages_per_step, page, QK_ROPE), F32),
            pltpu.SemaphoreType.DMA((2,)),
            pltpu.SemaphoreType.DMA((2,)),
            pltpu.VMEM((rows, 1), F32),
            pltpu.VMEM((rows, 1), F32),
            pltpu.VMEM((rows, kvl), F32),
        ],
    )
    return pl.pallas_call(
        kern,
        grid_spec=grid_spec,
        out_shape=jax.ShapeDtypeStruct((n_dec * rows, kvl), F32),
        compiler_params=_params(("arbitrary",)),
        name="paged_attention",
    )(page_table.reshape(-1), qcat, c_new, kr_new, cache_ckv, cache_krope)


def _ov_kernel(o_ref, w_ref, out_ref):
    out_ref[...] = jnp.dot(o_ref[...].astype(BF16), w_ref[...], preferred_element_type=F32)


def _value_up(o_lat_hm, w_uv_hm):
    n_heads, t, kvl = o_lat_hm.shape
    vh = w_uv_hm.shape[2]
    return pl.pallas_call(
        _ov_kernel,
        grid=(n_heads,),
        in_specs=[
            pl.BlockSpec((None, t, kvl), lambda h: (h, 0, 0)),
            pl.BlockSpec((None, kvl, vh), lambda h: (h, 0, 0)),
        ],
        out_specs=pl.BlockSpec((None, t, vh), lambda h: (h, 0, 0)),
        out_shape=jax.ShapeDtypeStruct((n_heads, t, vh), F32),
        compiler_params=_params(("arbitrary",)),
        name="value_up",
    )(o_lat_hm, w_uv_hm)


def _s5_weights(lam_re, lam_im, log_dt, b_re, b_im, c_re, c_im):
    lr = lam_re.astype(F32)
    li = lam_im.astype(F32)
    dt = jnp.exp(log_dt.astype(F32))[:, None]
    mag = jnp.exp(lr * dt)
    ang = li * dt
    ab_re = mag * jnp.cos(ang)
    ab_im = mag * jnp.sin(ang)
    den = lr * lr + li * li
    f_re = ((ab_re - 1.0) * lr + ab_im * li) / den
    f_im = (ab_im * lr - (ab_re - 1.0) * li) / den
    br = b_re.astype(F32)
    bi = b_im.astype(F32)
    bb_re = f_re[..., None] * br - f_im[..., None] * bi
    bb_im = f_re[..., None] * bi + f_im[..., None] * br
    g = lr.shape[0]
    gpc = S5_GROUPS_PER_CHUNK
    n_chunks = g // gpc
    eye = jnp.eye(gpc, dtype=F32)

    def b_block(bb):
        t = bb.reshape(n_chunks, gpc, SSM_STATE, SSM_GROUP).transpose(0, 1, 3, 2)
        blk = t[:, :, :, None, :] * eye[None, :, None, :, None]
        return blk.reshape(n_chunks, gpc * SSM_GROUP, gpc * SSM_STATE)

    def c_block(cc):
        t = cc.reshape(n_chunks, gpc, SSM_GROUP, SSM_STATE).transpose(0, 1, 3, 2)
        blk = t[:, :, :, None, :] * eye[None, :, None, :, None]
        return blk.reshape(n_chunks, gpc * SSM_STATE, gpc * SSM_GROUP)

    bblk = jnp.concatenate([b_block(bb_re), b_block(bb_im)], axis=2).astype(BF16)
    cblk = jnp.concatenate([c_block(c_re.astype(F32)), -c_block(c_im.astype(F32))], axis=1).astype(BF16)
    a_re = ab_re.reshape(n_chunks, 1, gpc * SSM_STATE)
    a_im = ab_im.reshape(n_chunks, 1, gpc * SSM_STATE)
    return bblk, cblk, a_re, a_im


def _rope_tables(pos, n_rep):
    inv = 1.0 / (ROPE_THETA ** (jnp.arange(0, QK_ROPE, 2, dtype=F32) / QK_ROPE))
    ang = pos.astype(F32)[:, None] * inv[None, :]
    cos = jnp.cos(ang)
    sin = jnp.sin(ang)
    length = pos.shape[0]
    zero_lo = jnp.zeros((length, QK_NOPE), F32)
    zero_hi = jnp.zeros((length, HEAD_PAD - QK_NOPE - QK_ROPE), F32)
    cos_k = jnp.concatenate([zero_lo, cos, cos, zero_hi], axis=1)
    sin_k = jnp.concatenate([zero_lo, sin, sin, zero_hi], axis=1)
    cos_q = jnp.concatenate([jnp.ones((length, QK_NOPE), F32), cos, cos, zero_hi], axis=1)
    if n_rep > 1:
        cos_k, sin_k, cos_q = (jnp.tile(a, (n_rep, 1)) for a in (cos_k, sin_k, cos_q))
    return cos_k, sin_k, cos_q


def _pad_heads(w, n_heads, width, offset=0):
    k = w.shape[0]
    w3 = w.reshape(k, n_heads, width)
    out = jnp.zeros((k, n_heads, HEAD_PAD), w.dtype).at[:, :, offset:offset + width].set(w3)
    return out.reshape(k, n_heads * HEAD_PAD)


def _swap_rope(w_rope):
    half = QK_ROPE // 2
    return jnp.concatenate([-w_rope[..., half:], w_rope[..., :half]], axis=-1)


def _mla_weights(p, b):
    n_heads = p["mla_w_uk"].shape[1]
    kvl = p["mla_w_uk"].shape[0]
    w_qb = p["mla_w_q_b"][b]
    ql = w_qb.shape[0]
    w3 = w_qb.reshape(ql, n_heads, QK_NOPE + QK_ROPE)
    zero_tail = jnp.zeros((ql, n_heads, HEAD_PAD - QK_NOPE - QK_ROPE), F32)
    wqb_p = jnp.concatenate([w3, zero_tail], axis=2).reshape(ql, n_heads * HEAD_PAD)
    wqb_s = jnp.concatenate([jnp.zeros((ql, n_heads, QK_NOPE), F32), _swap_rope(w3[..., QK_NOPE:]),
                             zero_tail], axis=2).reshape(ql, n_heads * HEAD_PAD)
    w_o = p["mla_w_o"][b]
    d_model = w_o.shape[1]
    w_o_p = jnp.zeros((n_heads, HEAD_PAD, d_model), F32).at[:, :V_HEAD].set(
        w_o.reshape(n_heads, V_HEAD, d_model)).reshape(n_heads * HEAD_PAD, d_model)
    del kvl
    return dict(wqa=p["mla_w_q_a"][b].astype(BF16), g_q=p["mla_g_q"][b][None],
                wqb_p=wqb_p.astype(BF16), wqb_s=wqb_s.astype(BF16),
                w_o=w_o.astype(BF16), w_o_p=w_o_p.astype(BF16))


def _shared_mla_weights(p):
    w_kv_a = p["mla_w_kv_a"]
    d_model = w_kv_a.shape[0]
    kvl, n_heads, _ = p["mla_w_uk"].shape
    w_rope = w_kv_a[:, kvl:]
    pad_lo = jnp.zeros((d_model, QK_NOPE), F32)
    pad_hi = jnp.zeros((d_model, HEAD_PAD - QK_NOPE - QK_ROPE), F32)
    wr = jnp.concatenate([pad_lo, w_rope, pad_hi], axis=1)
    wrs = jnp.concatenate([pad_lo, _swap_rope(w_rope), pad_hi], axis=1)
    wuk_p = _pad_heads(p["mla_w_uk"].reshape(kvl, n_heads * QK_NOPE), n_heads, QK_NOPE)
    wuv_p = _pad_heads(p["mla_w_uv"].reshape(kvl, n_heads * V_HEAD), n_heads, V_HEAD)
    width = kvl + LANES
    m_heads = jnp.zeros((n_heads, HEAD_PAD, width), F32)
    m_heads = m_heads.at[:, :QK_NOPE, :kvl].set(p["mla_w_uk"].transpose(1, 2, 0))
    m_heads = m_heads.at[:, QK_NOPE:QK_NOPE + QK_ROPE, kvl:kvl + QK_ROPE].set(
        jnp.broadcast_to(jnp.eye(QK_ROPE, dtype=F32), (n_heads, QK_ROPE, QK_ROPE)))
    return dict(wc=w_kv_a[:, :kvl].astype(BF16), g_kv=p["mla_g_kv"][None], wr=wr.astype(BF16),
                wrs=wrs.astype(BF16), wuk_p=wuk_p.astype(BF16), wuv_p=wuv_p.astype(BF16),
                m_heads=m_heads.astype(BF16), w_uv_hm=p["mla_w_uv"].transpose(1, 0, 2).astype(BF16))


def _tiles(t):
    big = t >= 4096
    return dict(tm=256 if big else min(t, 128), bm=512 if big else 64,
                tt_route=512 if big else min(t, 128), tt_scatter=256 if big else min(t, 128),
                tt_combine=128 if big else min(t, 128))


def _trunk(x, pos, h0, past, p, s5w, mla_shared, mla_layers):
    n_seq, seq, d_model = x.shape
    t = n_seq * seq
    cfg = _tiles(t)
    n_a = len(s5w)
    n_states = d_model // SSM_GROUP * SSM_STATE

    assert d_model == ROW_TILE * LANES, "MoE row buffers hold one (8, 128) tile per row"

    xt = x.transpose(1, 0, 2).reshape(t, d_model)
    lt = min(seq, 256)
    new_re, new_im = [], []
    for a in range(n_a):
        bblk, cblk, a_re, a_im = s5w[a]
        if h0 is None:
            h0r = jnp.zeros((n_seq, n_states), F32)
            h0i = jnp.zeros((n_seq, n_states), F32)
        else:
            h0r = h0[0][a].reshape(n_seq, n_states)
            h0i = h0[1][a].reshape(n_seq, n_states)
        z, hr, hi = _s5_scan(xt, bblk, cblk, a_re, a_im, p["ssm_d"][a][None], h0r, h0i,
                             n_seq=n_seq, lt=lt)
        new_re.append(hr.reshape(n_seq, d_model // SSM_GROUP, SSM_STATE))
        new_im.append(hi.reshape(n_seq, d_model // SSM_GROUP, SSM_STATE))
        proj = _proj_ln_router(z, p["ssm_w_glu_bf"][a], p["ssm_b_glu"][a][None], xt,
                               p["ln_mix_g"][a][None], p["ln_mix_b"][a][None],
                               p["moe_w_router"][a], p["moe_b_router"][a][None],
                               glu=True, tm=cfg["tm"])
        xt = yield proj

    xb = xt.reshape(seq, n_seq, d_model).transpose(1, 0, 2).reshape(t, d_model)
    tm = cfg["tm"]
    n_rep = 1 if seq >= tm else tm // seq
    cos_k, sin_k, cos_q = _rope_tables(pos, n_rep)
    prompt = past is None
    lat = _mla_latent(xb, mla_shared["wc"], mla_shared["g_kv"], mla_shared["wr"], mla_shared["wrs"],
                      cos_k, sin_k, mla_shared["wuk_p"], mla_shared["wuv_p"], tm=tm, with_kv=prompt)
    c_lat, kr128 = lat[0], lat[1]
    kr = kr128[:, QK_NOPE:QK_NOPE + QK_ROPE]
    kvl = c_lat.shape[1]
    n_heads = mla_shared["m_heads"].shape[0]
    zero_bias = jnp.zeros((1, d_model), F32)
    if not prompt:
        cache_ckv, cache_krope, page_table = past
        n_new = -(-seq // SUBLANES) * SUBLANES
        c_new = jnp.zeros((n_seq, n_new, kvl), F32).at[:, :seq].set(c_lat.reshape(n_seq, seq, kvl))
        kr_new = jnp.zeros((n_seq, n_new, QK_ROPE), F32).at[:, :seq].set(kr.reshape(n_seq, seq, QK_ROPE))
    for b, mw in enumerate(mla_layers):
        layer = n_a + b
        qp = _mla_queries(xb, mw["wqa"], mw["g_q"], mw["wqb_p"], mw["wqb_s"], cos_q, sin_k, tm=tm)
        if prompt:
            o = _flash_attention(qp, lat[2], lat[3], n_seq=n_seq, tq=min(seq, 512))
            w_o = mw["w_o_p"]
        else:
            qcat = _q_latent(qp, mla_shared["m_heads"])
            width = qcat.shape[2]
            qcat = qcat.reshape(n_heads, n_seq, seq, width).transpose(1, 0, 2, 3).reshape(
                n_seq * n_heads * seq, width)
            o_lat = _paged_attention(qcat, c_new, kr_new, cache_ckv, cache_krope, page_table,
                                     dec_seq=seq, n_heads=n_heads,
                                     pages_per_step=math.gcd(page_table.shape[1], PAGES_PER_STEP))
            o_hm = o_lat.reshape(n_seq, n_heads, seq, kvl).transpose(1, 0, 2, 3).reshape(n_heads, t, kvl)
            o = _value_up(o_hm, mla_shared["w_uv_hm"])
            o = o.transpose(1, 0, 2).reshape(t, n_heads * V_HEAD)
            w_o = mw["w_o"]
        proj = _proj_ln_router(o, w_o, zero_bias, xb, p["ln_mix_g"][layer][None],
                               p["ln_mix_b"][layer][None], p["moe_w_router"][layer],
                               p["moe_b_router"][layer][None], glu=False, tm=tm)
        xb = yield proj
    y = xb.reshape(n_seq, seq, d_model)
    return (y, jnp.stack(new_re), jnp.stack(new_im), c_lat.reshape(n_seq, seq, kvl),
            kr.reshape(n_seq, seq, QK_ROPE))


def kernel(x_prompt, x_sample, state_ssm_re, state_ssm_im, cache_ckv, cache_krope, page_table,
           ssm_lam_re, ssm_lam_im, ssm_log_dt, ssm_b_re, ssm_b_im, ssm_c_re, ssm_c_im, ssm_d,
           ssm_w_glu, ssm_b_glu, mla_w_kv_a, mla_g_kv, mla_w_uk, mla_w_uv, mla_w_q_a, mla_g_q,
           mla_w_q_b, mla_w_o, moe_w_router, moe_b_router, moe_w_gu, moe_b_gu, moe_w_dn, moe_b_dn,
           ln_mix_g, ln_mix_b, ln_ffn_g, ln_ffn_b):
    p = dict(ssm_d=ssm_d, ssm_b_glu=ssm_b_glu, ssm_w_glu_bf=ssm_w_glu.astype(BF16),
             mla_w_kv_a=mla_w_kv_a, mla_g_kv=mla_g_kv, mla_w_uk=mla_w_uk, mla_w_uv=mla_w_uv,
             mla_w_q_a=mla_w_q_a, mla_g_q=mla_g_q, mla_w_q_b=mla_w_q_b, mla_w_o=mla_w_o,
             moe_w_router=moe_w_router, moe_b_router=moe_b_router, moe_w_gu=moe_w_gu,
             moe_b_gu4=moe_b_gu[:, :, None, :], moe_w_dn=moe_w_dn, moe_b_dn4=moe_b_dn[:, :, None, :],
             ln_mix_g=ln_mix_g, ln_mix_b=ln_mix_b, ln_ffn_g=ln_ffn_g, ln_ffn_b=ln_ffn_b)
    n_a = ssm_lam_re.shape[0]
    s5w = [_s5_weights(ssm_lam_re[a], ssm_lam_im[a], ssm_log_dt[a], ssm_b_re[a], ssm_b_im[a],
                       ssm_c_re[a], ssm_c_im[a]) for a in range(n_a)]
    mla_shared = _shared_mla_weights(p)
    mla_layers = [_mla_weights(p, b) for b in range(mla_w_q_a.shape[0])]

    n_pages = page_table.shape[1]
    past_len = n_pages * cache_ckv.shape[1]
    pos_sample = past_len + jnp.arange(x_sample.shape[1])
    stream_s = _trunk(x_sample, pos_sample, (state_ssm_re, state_ssm_im),
                      (cache_ckv, cache_krope, page_table), p, s5w, mla_shared, mla_layers)
    pos_prompt = jnp.arange(x_prompt.shape[1])
    stream_p = _trunk(x_prompt, pos_prompt, None, None, p, s5w, mla_shared, mla_layers)
    cfg = _tiles(x_prompt.shape[0] * x_prompt.shape[1])
    tt_small = min(x_sample.shape[0] * x_sample.shape[1], 128)
    req_s, req_p = next(stream_s), next(stream_p)
    out_s = out_p = None
    for layer in range(moe_w_gu.shape[0]):
        new_p, new_s = _moe_pair(req_p, req_s, p, layer, bm=cfg["bm"], tt_route=cfg["tt_route"],
                                 tt_scatter=cfg["tt_scatter"], tt_combine=cfg["tt_combine"],
                                 tt_small=tt_small)
        try:
            req_s = stream_s.send(new_s)
        except StopIteration as done:
            out_s = done.value
        try:
            req_p = stream_p.send(new_p)
        except StopIteration as done:
            out_p = done.value
    return (out_p[0], out_s[0], out_p[1], out_p[2], out_p[3], out_p[4],
            out_s[1], out_s[2], out_s[3], out_s[4])
```

```python
import functools
import math

import jax
import jax.numpy as jnp
from jax import lax
from jax.experimental import pallas as pl
from jax.experimental.pallas import tpu as pltpu

F32 = jnp.float32
BF16 = jnp.bfloat16

SSM_GROUP = 16
SSM_STATE = 64
QK_NOPE = 64
QK_ROPE = 32
V_HEAD = 64
ROPE_THETA = 10000.0
ATTN_SCALE = (QK_NOPE + QK_ROPE) ** -0.5
SCALE_LOG2E = ATTN_SCALE * math.log2(math.e)
TOP_K = 4
SWIGLU_LIMIT = 7.0
SWIGLU_ALPHA = 1.702
LN_EPS = 1e-5
RMS_EPS = 1e-6
DEPTH = 4
DEEPNORM_ALPHA = (2 * DEPTH) ** 0.25

LANES = 128
SUBLANES = 8
ROW_TILE = SUBLANES
HEAD_PAD = LANES
PAGES_PER_STEP = 64
DENSE_EXPERTS_PER_STEP = 8
ZERO_ROWS = 64
DMA_UNROLL = 8
VMEM_LIMIT = 56 * 1024 * 1024

S5_CHUNK = LANES
S5_GROUPS_PER_CHUNK = S5_CHUNK // SSM_GROUP
S5_STATES_PER_CHUNK = S5_GROUPS_PER_CHUNK * SSM_STATE


def _params(sem, vmem=VMEM_LIMIT):
    return pltpu.CompilerParams(dimension_semantics=sem, vmem_limit_bytes=vmem)


def _layer_norm(x, g, b):
    mu = jnp.mean(x, axis=-1, keepdims=True)
    xc = x - mu
    var = jnp.mean(xc * xc, axis=-1, keepdims=True)
    return xc * lax.rsqrt(var + LN_EPS) * g + b


def _rms_norm(x, g):
    return x * lax.rsqrt(jnp.mean(x * x, axis=-1, keepdims=True) + RMS_EPS) * g


def _gelu_tanh(x):
    c = math.sqrt(2.0 / math.pi)
    return 0.5 * x * (1.0 + jnp.tanh(c * (x + 0.044715 * (x * x * x))))


def _s5_kernel(x_ref, bblk_ref, cblk_ref, are_ref, aim_ref, d_ref, h0r_ref, h0i_ref,
               z_ref, hr_out, hi_out, hbuf, st_re, st_im, *, n_seq, lt):
    i = pl.program_id(1)
    ns = S5_STATES_PER_CHUNK

    @pl.when(i == 0)
    def _():
        st_re[...] = h0r_ref[...]
        st_im[...] = h0i_ref[...]

    u = x_ref[...]
    hbuf[...] = jnp.dot(u.astype(BF16), bblk_ref[...], preferred_element_type=F32)
    a_re = jnp.broadcast_to(are_ref[...], (n_seq, ns))
    a_im = jnp.broadcast_to(aim_ref[...], (n_seq, ns))

    def step(l, carry):
        h_re, h_im = carry
        r0 = pl.multiple_of(l * n_seq, n_seq)
        bu_re = hbuf[pl.ds(r0, n_seq), 0:ns]
        bu_im = hbuf[pl.ds(r0, n_seq), ns:2 * ns]
        n_re = a_re * h_re - a_im * h_im + bu_re
        n_im = a_re * h_im + a_im * h_re + bu_im
        hbuf[pl.ds(r0, n_seq), 0:ns] = n_re
        hbuf[pl.ds(r0, n_seq), ns:2 * ns] = n_im
        return n_re, n_im

    h_re, h_im = lax.fori_loop(0, lt, step, (st_re[...], st_im[...]), unroll=min(lt, 8))
    st_re[...] = h_re
    st_im[...] = h_im

    y = jnp.dot(hbuf[...].astype(BF16), cblk_ref[...], preferred_element_type=F32)
    z_ref[...] = _gelu_tanh(y + d_ref[...] * u)

    @pl.when(i == pl.num_programs(1) - 1)
    def _():
        hr_out[...] = h_re
        hi_out[...] = h_im


def _s5_scan(x_tm, bblk, cblk, a_re, a_im, d_skip, h0_re, h0_im, *, n_seq, lt):
    rows, d_model = x_tm.shape
    n_chunks = d_model // S5_CHUNK
    ns = S5_STATES_PER_CHUNK
    seq = rows // n_seq
    tile = lt * n_seq
    kern = functools.partial(_s5_kernel, n_seq=n_seq, lt=lt)
    return pl.pallas_call(
        kern,
        grid=(n_chunks, seq // lt),
        in_specs=[
            pl.BlockSpec((tile, S5_CHUNK), lambda c, i: (i, c)),
            pl.BlockSpec((None, S5_CHUNK, 2 * ns), lambda c, i: (c, 0, 0)),
            pl.BlockSpec((None, 2 * ns, S5_CHUNK), lambda c, i: (c, 0, 0)),
            pl.BlockSpec((None, 1, ns), lambda c, i: (c, 0, 0)),
            pl.BlockSpec((None, 1, ns), lambda c, i: (c, 0, 0)),
            pl.BlockSpec((1, S5_CHUNK), lambda c, i: (0, c)),
            pl.BlockSpec((n_seq, ns), lambda c, i: (0, c)),
            pl.BlockSpec((n_seq, ns), lambda c, i: (0, c)),
        ],
        out_specs=[
            pl.BlockSpec((tile, S5_CHUNK), lambda c, i: (i, c)),
            pl.BlockSpec((n_seq, ns), lambda c, i: (0, c)),
            pl.BlockSpec((n_seq, ns), lambda c, i: (0, c)),
        ],
        out_shape=[
            jax.ShapeDtypeStruct((rows, d_model), F32),
            jax.ShapeDtypeStruct((n_seq, n_chunks * ns), F32),
            jax.ShapeDtypeStruct((n_seq, n_chunks * ns), F32),
        ],
        scratch_shapes=[
            pltpu.VMEM((tile, 2 * ns), F32),
            pltpu.VMEM((n_seq, ns), F32),
            pltpu.VMEM((n_seq, ns), F32),
        ],
        compiler_params=_params(("arbitrary", "arbitrary")),
        name="s5_scan",
    )(x_tm, bblk, cblk, a_re, a_im, d_skip, h0_re, h0_im)


def _split_bf16(v):
    hi = v.astype(BF16)
    return hi, (v - hi.astype(F32)).astype(BF16)


def _proj_ln_router_kernel(a_ref, w_ref, b_ref, x_ref, g_ref, beta_ref, wrh_ref, wrl_ref, br_ref,
                           x1_ref, xrow_ref, lg_ref, *, glu):
    h = jnp.dot(a_ref[...].astype(BF16), w_ref[...], preferred_element_type=F32) + b_ref[...]
    if glu:
        d = h.shape[1] // 2
        mix = h[:, :d] * jax.nn.sigmoid(h[:, d:])
    else:
        mix = h
    x1 = _layer_norm(DEEPNORM_ALPHA * x_ref[...] + mix, g_ref[...], beta_ref[...])
    x1_ref[...] = x1
    tm, d_model = x1.shape
    for c in range(d_model // LANES):
        xrow_ref[pl.ds(c, tm, stride=ROW_TILE), :] = x1[:, c * LANES:(c + 1) * LANES]
    xh, xl = _split_bf16(x1)
    lg_ref[...] = (jnp.dot(xh, wrh_ref[...], preferred_element_type=F32)
                   + jnp.dot(xl, wrh_ref[...], preferred_element_type=F32)
                   + jnp.dot(xh, wrl_ref[...], preferred_element_type=F32)) + br_ref[...]


def _proj_ln_router(a, w, b, x, g, beta, wr, br, *, glu, tm):
    t, k = a.shape
    nw = w.shape[1]
    d = x.shape[1]
    ne = wr.shape[1]
    wrh, wrl = _split_bf16(wr.astype(F32))
    kern = functools.partial(_proj_ln_router_kernel, glu=glu)
    return pl.pallas_call(
        kern,
        grid=(t // tm,),
        in_specs=[
            pl.BlockSpec((tm, k), lambda i: (i, 0)),
            pl.BlockSpec((k, nw), lambda i: (0, 0)),
            pl.BlockSpec((1, nw), lambda i: (0, 0)),
            pl.BlockSpec((tm, d), lambda i: (i, 0)),
            pl.BlockSpec((1, d), lambda i: (0, 0)),
            pl.BlockSpec((1, d), lambda i: (0, 0)),
            pl.BlockSpec((d, ne), lambda i: (0, 0)),
            pl.BlockSpec((d, ne), lambda i: (0, 0)),
            pl.BlockSpec((1, ne), lambda i: (0, 0)),
        ],
        out_specs=[
            pl.BlockSpec((tm, d), lambda i: (i, 0)),
            pl.BlockSpec((tm * ROW_TILE, LANES), lambda i: (i, 0)),
            pl.BlockSpec((tm, ne), lambda i: (i, 0)),
        ],
        out_shape=[
            jax.ShapeDtypeStruct((t, d), F32),
            jax.ShapeDtypeStruct((t * ROW_TILE, LANES), F32),
            jax.ShapeDtypeStruct((t, ne), F32),
        ],
        compiler_params=_params(("arbitrary",)),
        name="proj_ln_router",
    )(a, w, b, x, g, beta, wrh, wrl, br)


def _route_kernel(lg_ref, tri_ref, eidx_ref, rank_ref, gate_ref, cnt_ref, base_scr):
    i = pl.program_id(0)

    @pl.when(i == 0)
    def _():
        base_scr[...] = jnp.zeros_like(base_scr)

    l = lg_ref[...]
    tt, ne = l.shape
    lane = lax.broadcasted_iota(jnp.int32, (tt, ne), 1).astype(F32)
    tops, idxs, hots = [], [], []
    for _ in range(TOP_K):
        m = jnp.max(l, axis=1, keepdims=True)
        idx = jnp.min(jnp.where(l == m, lane, float(ne)), axis=1, keepdims=True)
        hot = lane == idx
        tops.append(m)
        idxs.append(idx.astype(jnp.int32))
        hots.append(hot)
        l = jnp.where(hot, -jnp.inf, l)
    sel = jnp.zeros((tt, ne), F32)
    for hot in hots:
        sel = sel + jnp.where(hot, 1.0, 0.0)
    prefix = jnp.dot(tri_ref[...], sel.astype(BF16), preferred_element_type=F32) + base_scr[...]
    es = [jnp.exp(m - tops[0]) for m in tops]
    den = es[0] + es[1] + es[2] + es[3]
    wide = lax.broadcasted_iota(jnp.int32, (tt, LANES), 1)
    eidx_w = jnp.zeros((tt, LANES), jnp.int32)
    rank_w = jnp.zeros((tt, LANES), jnp.int32)
    gate_w = jnp.zeros((tt, LANES), F32)
    for k in range(TOP_K):
        rank_k = jnp.sum(jnp.where(hots[k], prefix, 0.0), axis=1, keepdims=True).astype(jnp.int32)
        eidx_w = jnp.where(wide == k, idxs[k], eidx_w)
        rank_w = jnp.where(wide == k, rank_k, rank_w)
        gate_w = jnp.where(wide == k, es[k] / den, gate_w)
    eidx_ref[...] = eidx_w
    rank_ref[...] = rank_w
    gate_ref[...] = gate_w
    base_scr[...] = base_scr[...] + jnp.sum(sel, axis=0, keepdims=True)
    cnt_ref[...] = base_scr[...].astype(jnp.int32)


def _route(logits, *, tt):
    t, ne = logits.shape
    tri = jnp.tri(tt, k=-1, dtype=BF16)
    return pl.pallas_call(
        _route_kernel,
        grid=(t // tt,),
        in_specs=[
            pl.BlockSpec((tt, ne), lambda i: (i, 0)),
            pl.BlockSpec((tt, tt), lambda i: (0, 0)),
        ],
        out_specs=[
            pl.BlockSpec((tt, LANES), lambda i: (i, 0)),
            pl.BlockSpec((tt, LANES), lambda i: (i, 0)),
            pl.BlockSpec((tt, LANES), lambda i: (i, 0)),
            pl.BlockSpec((1, ne), lambda i: (0, 0)),
        ],
        out_shape=[
            jax.ShapeDtypeStruct((t, LANES), jnp.int32),
            jax.ShapeDtypeStruct((t, LANES), jnp.int32),
            jax.ShapeDtypeStruct((t, LANES), F32),
            jax.ShapeDtypeStruct((1, ne), jnp.int32),
        ],
        scratch_shapes=[pltpu.VMEM((1, ne), F32)],
        compiler_params=_params(("arbitrary",)),
        name="moe_route",
    )(logits, tri)


def _tile_rows(row):
    return pl.ds(pl.multiple_of(row * ROW_TILE, ROW_TILE), ROW_TILE)


def _scatter_kernel(dest_ref, pad_ref, x_ref, rows_out, zero_rows, sem, zsem, zsem_chunk):
    i = pl.program_id(0)
    n_assign = dest_ref.shape[1]
    tt = n_assign // TOP_K

    def row_copy(src_row, dst_row):
        return pltpu.make_async_copy(x_ref.at[_tile_rows(src_row)], rows_out.at[_tile_rows(dst_row)], sem)

    def drain(j, carry):
        row_copy(0, 0).wait()
        return carry

    for k in range(TOP_K):
        def issue(g, carry, k=k):
            for u in range(DMA_UNROLL):
                t = g * DMA_UNROLL + u
                row_copy(t, dest_ref[0, k * tt + t]).start(priority=u % 2)
            return carry
        lax.fori_loop(0, tt // DMA_UNROLL, issue, 0)

    @pl.when(i == pl.num_programs(0) - 1)
    def _():
        zero_rows[...] = jnp.zeros_like(zero_rows)

        def zero_copy(dst_row):
            return pltpu.make_async_copy(zero_rows.at[pl.ds(0, ROW_TILE)],
                                         rows_out.at[_tile_rows(dst_row)], zsem)

        def zero_chunk_copy(dst_row):
            dst = pl.ds(pl.multiple_of(dst_row * ROW_TILE, ROW_TILE), ZERO_ROWS * ROW_TILE)
            return pltpu.make_async_copy(zero_rows, rows_out.at[dst], zsem_chunk)

        def per_range(r, counts):
            n_single, n_chunk = counts
            lo = pad_ref[0, r]
            hi = pad_ref[1, r]
            chunks = (hi - lo) // ZERO_ROWS

            def chunk_body(c, carry):
                zero_chunk_copy(lo + c * ZERO_ROWS).start()
                return carry

            def single_body(row, carry):
                zero_copy(row).start()
                return carry

            lax.fori_loop(0, chunks, chunk_body, 0)
            lax.fori_loop(lo + chunks * ZERO_ROWS, hi, single_body, 0)
            return n_single + (hi - lo) - chunks * ZERO_ROWS, n_chunk + chunks

        n_single, n_chunk = lax.fori_loop(0, pad_ref.shape[1], per_range, (0, 0))

        def chunk_drain(c, carry):
            zero_chunk_copy(0).wait()
            return carry

        def single_drain(r, carry):
            zero_copy(0).wait()
            return carry

        lax.fori_loop(0, n_chunk, chunk_drain, 0)
        lax.fori_loop(0, n_single, single_drain, 0)

    lax.fori_loop(0, n_assign, drain, 0, unroll=DMA_UNROLL)


def _k_major_tiles(dest, tt):
    t = dest.shape[0]
    return dest.reshape(t // tt, tt, TOP_K).transpose(0, 2, 1).reshape(t // tt, 1, TOP_K * tt)


def _moe_scatter(x_rows_src, dest, pad_ranges, n_rows, *, tt):
    t = x_rows_src.shape[0] // ROW_TILE
    dest2 = _k_major_tiles(dest, tt)
    return pl.pallas_call(
        _scatter_kernel,
        grid=(t // tt,),
        in_specs=[
            pl.BlockSpec((None, 1, tt * TOP_K), lambda i: (i, 0, 0), memory_space=pltpu.SMEM),
            pl.BlockSpec(pad_ranges.shape, lambda i: (0, 0), memory_space=pltpu.SMEM),
            pl.BlockSpec((tt * ROW_TILE, LANES), lambda i: (i, 0)),
        ],
        out_specs=pl.BlockSpec(memory_space=pl.ANY),
        out_shape=jax.ShapeDtypeStruct((n_rows * ROW_TILE, LANES), F32),
        scratch_shapes=[pltpu.VMEM((ZERO_ROWS * ROW_TILE, LANES), F32), pltpu.SemaphoreType.DMA(()),
                        pltpu.SemaphoreType.DMA(()), pltpu.SemaphoreType.DMA(())],
        compiler_params=_params(("arbitrary",)),
        name="moe_scatter",
    )(dest2, pad_ranges, x_rows_src)


def _ffn_kernel(be_ref, nu_ref, x_ref, xs_ref, wgu_ref, bgu_ref, wdn_ref, bdn_ref, y_ref, ys_ref,
                wgu_bf, wdn_bf, x_bf):
    b = pl.program_id(0)
    prev = be_ref[jnp.maximum(b - 1, 0)]
    changed = jnp.logical_or(b == 0, be_ref[b] != prev)
    bm, d_model = x_bf.shape
    n_lane_tiles = d_model // LANES

    def expert_ffn(xb):
        de = wdn_bf.shape[0]
        h = jnp.dot(xb, wgu_bf[...], preferred_element_type=F32) + bgu_ref[...]
        g = jnp.minimum(h[:, :de], SWIGLU_LIMIT)
        up = jnp.clip(h[:, de:], -SWIGLU_LIMIT, SWIGLU_LIMIT)
        act = (up + 1.0) * (g * jax.nn.sigmoid(SWIGLU_ALPHA * g))
        return jnp.dot(act.astype(BF16), wdn_bf[...], preferred_element_type=F32) + bdn_ref[...]

    @pl.when(changed)
    def _():
        wgu_bf[...] = wgu_ref[...].astype(BF16)
        wdn_bf[...] = wdn_ref[...].astype(BF16)
        ys_ref[...] = expert_ffn(xs_ref[...])

    @pl.when(b < nu_ref[0])
    def _():
        for c in range(n_lane_tiles):
            x_bf[:, c * LANES:(c + 1) * LANES] = x_ref[pl.ds(c, bm, stride=ROW_TILE), :].astype(BF16)
        y = expert_ffn(x_bf[...])
        for c in range(n_lane_tiles):
            y_ref[pl.ds(c, bm, stride=ROW_TILE), :] = y[:, c * LANES:(c + 1) * LANES]

    @pl.when(b >= nu_ref[0])
    def _():
        y_ref[...] = jnp.zeros_like(y_ref)


def _moe_ffn(x_rows, xs_bf, block_e, n_used, w_gu, b_gu, w_dn, b_dn, layer, *, bm):
    n_rows = x_rows.shape[0] // ROW_TILE
    ts = xs_bf.shape[0]
    ne = w_dn.shape[1]
    d, de = w_dn.shape[3], w_dn.shape[2]
    grid_spec = pltpu.PrefetchScalarGridSpec(
        num_scalar_prefetch=2,
        grid=(n_rows // bm,),
        in_specs=[
            pl.BlockSpec((bm * ROW_TILE, LANES), lambda b, be, nu: (jnp.minimum(b, nu[0] - 1), 0)),
            pl.BlockSpec((ts, d), lambda b, be, nu: (0, 0)),
            pl.BlockSpec((None, None, d, 2 * de), lambda b, be, nu: (layer, be[b], 0, 0)),
            pl.BlockSpec((None, None, 1, 2 * de), lambda b, be, nu: (layer, be[b], 0, 0)),
            pl.BlockSpec((None, None, de, d), lambda b, be, nu: (layer, be[b], 0, 0)),
            pl.BlockSpec((None, None, 1, d), lambda b, be, nu: (layer, be[b], 0, 0)),
        ],
        out_specs=[
            pl.BlockSpec((bm * ROW_TILE, LANES), lambda b, be, nu: (b, 0)),
            pl.BlockSpec((None, ts, d), lambda b, be, nu: (be[b], 0, 0)),
        ],
        scratch_shapes=[pltpu.VMEM((d, 2 * de), BF16), pltpu.VMEM((de, d), BF16),
                        pltpu.VMEM((bm, d), BF16)],
    )
    return pl.pallas_call(
        _ffn_kernel,
        grid_spec=grid_spec,
        out_shape=[jax.ShapeDtypeStruct((n_rows * ROW_TILE, LANES), F32),
                   jax.ShapeDtypeStruct((ne, ts, d), F32)],
        compiler_params=_params(("arbitrary",)),
        name="moe_ffn",
    )(block_e, n_used, x_rows, xs_bf, w_gu, b_gu, w_dn, b_dn)


def _dense_combine_kernel(gate_ref, ys_ref, x_ref, g_ref, beta_ref, out_ref, acc):
    e = pl.program_id(0)

    @pl.when(e == 0)
    def _():
        acc[...] = jnp.zeros_like(acc)

    total = acc[...]
    for j in range(gate_ref.shape[0]):
        gate = gate_ref[j]
        total = total + jnp.where(gate != 0.0, gate * ys_ref[j], 0.0)
    acc[...] = total

    @pl.when(e == pl.num_programs(0) - 1)
    def _():
        out_ref[...] = _layer_norm(DEEPNORM_ALPHA * x_ref[...] + total, g_ref[...], beta_ref[...])


def _moe_dense_combine(ys, gate_dense, x1, g, beta):
    ne, ts, d = ys.shape
    per_step = math.gcd(ne, DENSE_EXPERTS_PER_STEP)
    return pl.pallas_call(
        _dense_combine_kernel,
        grid=(ne // per_step,),
        in_specs=[
            pl.BlockSpec((per_step, ts, 1), lambda e: (e, 0, 0)),
            pl.BlockSpec((per_step, ts, d), lambda e: (e, 0, 0)),
            pl.BlockSpec((ts, d), lambda e: (0, 0)),
            pl.BlockSpec((1, d), lambda e: (0, 0)),
            pl.BlockSpec((1, d), lambda e: (0, 0)),
        ],
        out_specs=pl.BlockSpec((ts, d), lambda e: (0, 0)),
        out_shape=jax.ShapeDtypeStruct((ts, d), F32),
        scratch_shapes=[pltpu.VMEM((ts, d), F32)],
        compiler_params=_params(("arbitrary",)),
        name="moe_dense_combine",
    )(gate_dense, ys, x1, g, beta)


def _combine_kernel(dcur_ref, dnxt_ref, gate_ref, x_ref, g_ref, beta_ref, yrows, out_ref, buf, sem):
    i = pl.program_id(0)
    n_steps = pl.num_programs(0)
    n_assign = dcur_ref.shape[1]
    tt = n_assign // TOP_K

    def row_copy(dref, k, t, slot):
        return pltpu.make_async_copy(yrows.at[_tile_rows(dref[0, k * tt + t])],
                                     buf.at[slot, k, _tile_rows(t)], sem.at[slot])

    def issue(dref, slot):
        for k in range(TOP_K):
            def body(g, carry, k=k):
                for u in range(DMA_UNROLL):
                    row_copy(dref, k, g * DMA_UNROLL + u, slot).start(priority=u % 2)
                return carry
            lax.fori_loop(0, tt // DMA_UNROLL, body, 0)

    @pl.when(i == 0)
    def _():
        issue(dcur_ref, 0)

    @pl.when(i + 1 < n_steps)
    def _():
        issue(dnxt_ref, (i + 1) % 2)

    slot = i % 2

    def drain(j, carry):
        row_copy(dcur_ref, 0, 0, slot).wait()
        return carry

    lax.fori_loop(0, n_assign, drain, 0, unroll=DMA_UNROLL)

    gates = gate_ref[...]
    d_model = x_ref.shape[1]
    pieces = []
    for c in range(d_model // LANES):
        f = gates[:, 0:1] * buf.at[slot, 0][pl.ds(c, tt, stride=ROW_TILE), :]
        for k in range(1, TOP_K):
            f = f + gates[:, k:k + 1] * buf.at[slot, k][pl.ds(c, tt, stride=ROW_TILE), :]
        pieces.append(DEEPNORM_ALPHA * x_ref[:, c * LANES:(c + 1) * LANES] + f)
    out_ref[...] = _layer_norm(jnp.concatenate(pieces, axis=1), g_ref[...], beta_ref[...])


def _moe_combine(y_rows, dest, gate_w, x1, g, beta, *, tt):
    t, d = x1.shape
    n_steps = t // tt
    dest2 = _k_major_tiles(dest, tt)
    return pl.pallas_call(
        _combine_kernel,
        grid=(n_steps,),
        in_specs=[
            pl.BlockSpec((None, 1, tt * TOP_K), lambda i: (i, 0, 0), memory_space=pltpu.SMEM),
            pl.BlockSpec((None, 1, tt * TOP_K), lambda i: (jnp.minimum(i + 1, n_steps - 1), 0, 0),
                         memory_space=pltpu.SMEM),
            pl.BlockSpec((tt, LANES), lambda i: (i, 0)),
            pl.BlockSpec((tt, d), lambda i: (i, 0)),
            pl.BlockSpec((1, d), lambda i: (0, 0)),
            pl.BlockSpec((1, d), lambda i: (0, 0)),
            pl.BlockSpec(memory_space=pl.ANY),
        ],
        out_specs=pl.BlockSpec((tt, d), lambda i: (i, 0)),
        out_shape=jax.ShapeDtypeStruct((t, d), F32),
        scratch_shapes=[pltpu.VMEM((2, TOP_K, tt * ROW_TILE, LANES), F32), pltpu.SemaphoreType.DMA((2,))],
        compiler_params=_params(("arbitrary",)),
        name="moe_combine",
    )(dest2, dest2, gate_w, x1, g, beta, y_rows)


def _moe_pair(big, small, p, layer, *, bm, tt_route, tt_scatter, tt_combine, tt_small):
    x1, x1_rows, logits = big
    xs1, _, logits_s = small
    t, d = x1.shape
    ne = logits.shape[1]
    eidx_w, rank_w, gate_w, cnt = _route(logits, tt=tt_route)
    eidx_s, _, gate_s, _ = _route(logits_s, tt=tt_small)
    experts = jnp.arange(ne, dtype=jnp.int32)
    gate_dense = jnp.sum(jnp.where(eidx_s[:, :TOP_K, None] == experts, gate_s[:, :TOP_K, None], 0.0), axis=1)
    gate_dense = gate_dense.T[:, :, None]
    cnt = cnt[0]
    padded = jnp.maximum((cnt + bm - 1) // bm, 1) * bm
    pend = jnp.cumsum(padded)
    pstart = pend - padded
    onehot_start = jnp.where(eidx_w[:, :TOP_K, None] == experts, pstart, 0)
    dest = (jnp.sum(onehot_start, axis=-1) + rank_w[:, :TOP_K]).astype(jnp.int32)
    n_blocks = (t * TOP_K) // bm + ne
    n_used = pend[-1] // bm
    blk = jnp.arange(n_blocks, dtype=jnp.int32)
    be = jnp.minimum(jnp.sum(pend[None, :] <= blk[:, None] * bm, axis=1), ne - 1).astype(jnp.int32)
    last_e = jnp.sum(jnp.where(blk == n_used - 1, be, 0))
    be = jnp.where(blk < n_used, be, last_e)
    pad_lo = jnp.concatenate([pstart + cnt, pend[-1:]])
    pad_hi = jnp.concatenate([pend, jnp.full((1,), n_blocks * bm, pend.dtype)])
    pad_ranges = jnp.stack([pad_lo, pad_hi]).astype(jnp.int32)
    x_rows = _moe_scatter(x1_rows, dest, pad_ranges, n_blocks * bm, tt=tt_scatter)
    y_rows, ys = _moe_ffn(x_rows, xs1.astype(BF16), be, n_used.reshape(1).astype(jnp.int32),
                          p["moe_w_gu"], p["moe_b_gu4"], p["moe_w_dn"], p["moe_b_dn4"], layer, bm=bm)
    ln_g, ln_b = p["ln_ffn_g"][layer][None], p["ln_ffn_b"][layer][None]
    out_big = _moe_combine(y_rows, dest, gate_w, x1, ln_g, ln_b, tt=tt_combine)
    out_small = _moe_dense_combine(ys, gate_dense, xs1, ln_g, ln_b)
    return out_big, out_small


def _latent_kernel(x_ref, wc_ref, g_ref, wr_ref, wrs_ref, cos_ref, sin_ref, *rest, with_kv):
    if with_kv:
        wuk_ref, wuv_ref, c_ref, kr_ref, kp_ref, vp_ref = rest
    else:
        c_ref, kr_ref = rest
    xb = x_ref[...].astype(BF16)
    kv = jnp.dot(xb, wc_ref[...], preferred_element_type=F32)
    c = _rms_norm(kv, g_ref[...])
    r = jnp.dot(xb, wr_ref[...], preferred_element_type=F32)
    rs = jnp.dot(xb, wrs_ref[...], preferred_element_type=F32)
    kr = r * cos_ref[...] + rs * sin_ref[...]
    c_ref[...] = c
    kr_ref[...] = kr
    if with_kv:
        cb = c.astype(BF16)
        kn = jnp.dot(cb, wuk_ref[...], preferred_element_type=F32)
        n_heads = kn.shape[1] // HEAD_PAD
        for h in range(n_heads):
            sl = slice(h * HEAD_PAD, (h + 1) * HEAD_PAD)
            kp_ref[:, sl] = (kn[:, sl] + kr).astype(BF16)
        vp_ref[...] = jnp.dot(cb, wuv_ref[...], preferred_element_type=F32).astype(BF16)


def _mla_latent(x, wc, g_kv, wr, wrs, cos_k, sin_k, wuk_p, wuv_p, *, tm, with_kv):
    t, d = x.shape
    kvl = wc.shape[1]
    n_tab = cos_k.shape[0] // tm
    in_specs = [
        pl.BlockSpec((tm, d), lambda i: (i, 0)),
        pl.BlockSpec((d, kvl), lambda i: (0, 0)),
        pl.BlockSpec((1, kvl), lambda i: (0, 0)),
        pl.BlockSpec((d, HEAD_PAD), lambda i: (0, 0)),
        pl.BlockSpec((d, HEAD_PAD), lambda i: (0, 0)),
        pl.BlockSpec((tm, HEAD_PAD), lambda i: (i % n_tab, 0)),
        pl.BlockSpec((tm, HEAD_PAD), lambda i: (i % n_tab, 0)),
    ]
    out_specs = [
        pl.BlockSpec((tm, kvl), lambda i: (i, 0)),
        pl.BlockSpec((tm, HEAD_PAD), lambda i: (i, 0)),
    ]
    out_shape = [jax.ShapeDtypeStruct((t, kvl), F32), jax.ShapeDtypeStruct((t, HEAD_PAD), F32)]
    args = [x, wc, g_kv, wr, wrs, cos_k, sin_k]
    if with_kv:
        hp = wuk_p.shape[1]
        in_specs += [pl.BlockSpec((kvl, hp), lambda i: (0, 0)), pl.BlockSpec((kvl, hp), lambda i: (0, 0))]
        out_specs += [pl.BlockSpec((tm, hp), lambda i: (i, 0)), pl.BlockSpec((tm, hp), lambda i: (i, 0))]
        out_shape += [jax.ShapeDtypeStruct((t, hp), BF16), jax.ShapeDtypeStruct((t, hp), BF16)]
        args += [wuk_p, wuv_p]
    return pl.pallas_call(
        functools.partial(_latent_kernel, with_kv=with_kv),
        grid=(t // tm,),
        in_specs=in_specs,
        out_specs=out_specs,
        out_shape=out_shape,
        compiler_params=_params(("arbitrary",)),
        name="mla_latent",
    )(*args)


def _query_kernel(x_ref, wqa_ref, g_ref, wqb_ref, wqs_ref, cos_ref, sin_ref, q_ref):
    xb = x_ref[...].astype(BF16)
    cq = _rms_norm(jnp.dot(xb, wqa_ref[...], preferred_element_type=F32), g_ref[...]).astype(BF16)
    q = jnp.dot(cq, wqb_ref[...], preferred_element_type=F32)
    qs = jnp.dot(cq, wqs_ref[...], preferred_element_type=F32)
    cos = cos_ref[...]
    sin = sin_ref[...]
    n_heads = q.shape[1] // HEAD_PAD
    for h in range(n_heads):
        sl = slice(h * HEAD_PAD, (h + 1) * HEAD_PAD)
        q_ref[:, sl] = (q[:, sl] * cos + qs[:, sl] * sin).astype(BF16)


def _mla_queries(x, wqa, g_q, wqb_p, wqb_s, cos_q, sin_q, *, tm):
    t, d = x.shape
    ql = wqa.shape[1]
    hp = wqb_p.shape[1]
    n_tab = cos_q.shape[0] // tm
    return pl.pallas_call(
        _query_kernel,
        grid=(t // tm,),
        in_specs=[
            pl.BlockSpec((tm, d), lambda i: (i, 0)),
            pl.BlockSpec((d, ql), lambda i: (0, 0)),
            pl.BlockSpec((1, ql), lambda i: (0, 0)),
            pl.BlockSpec((ql, hp), lambda i: (0, 0)),
            pl.BlockSpec((ql, hp), lambda i: (0, 0)),
            pl.BlockSpec((tm, HEAD_PAD), lambda i: (i % n_tab, 0)),
            pl.BlockSpec((tm, HEAD_PAD), lambda i: (i % n_tab, 0)),
        ],
        out_specs=pl.BlockSpec((tm, hp), lambda i: (i, 0)),
        out_shape=jax.ShapeDtypeStruct((t, hp), BF16),
        compiler_params=_params(("arbitrary",)),
        name="mla_queries",
    )(x, wqa, g_q, wqb_p, wqb_s, cos_q, sin_q)


def _flash_kernel(q_ref, k_ref, v_ref, o_ref, *, tq):
    seq = q_ref.shape[0]
    n_tiles = seq // tq
    row = lax.broadcasted_iota(jnp.int32, (tq, tq), 0)
    col = lax.broadcasted_iota(jnp.int32, (tq, tq), 1)
    for i in range(n_tiles):
        q = q_ref[i * tq:(i + 1) * tq, :]
        m = jnp.full((tq, 1), -jnp.inf, F32)
        l = jnp.zeros((tq, 1), F32)
        acc = jnp.zeros((tq, HEAD_PAD), F32)
        for j in range(i + 1):
            k = k_ref[j * tq:(j + 1) * tq, :]
            v = v_ref[j * tq:(j + 1) * tq, :]
            s = lax.dot_general(q, k, (((1,), (1,)), ((), ())), preferred_element_type=F32)
            if j == i:
                s = jnp.where(col <= row, s, -jnp.inf)
            m_new = jnp.maximum(m, jnp.max(s, axis=1, keepdims=True))
            alpha = jnp.exp2((m - m_new) * SCALE_LOG2E)
            pr = jnp.exp2((s - m_new) * SCALE_LOG2E)
            l = alpha * l + jnp.sum(pr, axis=1, keepdims=True)
            acc = alpha * acc + jnp.dot(pr.astype(BF16), v, preferred_element_type=F32)
            m = m_new
        o_ref[i * tq:(i + 1) * tq, :] = (acc / l).astype(o_ref.dtype)


def _flash_attention(qp, kp, vp, *, n_seq, tq):
    t, hp = qp.shape
    seq = t // n_seq
    n_heads = hp // HEAD_PAD
    q3 = qp.reshape(n_seq, seq, hp)
    k3 = kp.reshape(n_seq, seq, hp)
    v3 = vp.reshape(n_seq, seq, hp)
    spec = pl.BlockSpec((None, seq, HEAD_PAD), lambda n, h: (n, 0, h))
    out = pl.pallas_call(
        functools.partial(_flash_kernel, tq=tq),
        grid=(n_seq, n_heads),
        in_specs=[spec, spec, spec],
        out_specs=spec,
        out_shape=jax.ShapeDtypeStruct((n_seq, seq, hp), BF16),
        compiler_params=_params(("arbitrary", "arbitrary")),
        name="flash_attention",
    )(q3, k3, v3)
    return out.reshape(t, hp)


def _qlat_kernel(q_ref, m_ref, o_ref):
    o_ref[...] = jnp.dot(q_ref[...], m_ref[...], preferred_element_type=F32).astype(o_ref.dtype)


def _q_latent(qp, m_heads):
    t, hp = qp.shape
    n_heads, _, width = m_heads.shape
    return pl.pallas_call(
        _qlat_kernel,
        grid=(n_heads,),
        in_specs=[
            pl.BlockSpec((t, HEAD_PAD), lambda h: (0, h)),
            pl.BlockSpec((None, HEAD_PAD, width), lambda h: (h, 0, 0)),
        ],
        out_specs=pl.BlockSpec((None, t, width), lambda h: (h, 0, 0)),
        out_shape=jax.ShapeDtypeStruct((n_heads, t, width), BF16),
        compiler_params=_params(("arbitrary",)),
        name="q_latent",
    )(qp, m_heads)


def _paged_kernel(pt_ref, q_ref, cnew_ref, rnew_ref, ckv_hbm, kr_hbm, o_ref,
                  cbuf, rbuf, sem_c, sem_r, m_scr, l_scr, acc_scr,
                  *, n_chunks, pages_per_step, n_pages, dec_seq, kvl):
    s = pl.program_id(0)
    n_steps = pl.num_programs(0)
    j = s % n_chunks
    page = cbuf.shape[2]

    def copies(step, slot, p):
        pg = pt_ref[(step // n_chunks) * n_pages + (step % n_chunks) * pages_per_step + p]
        return (pltpu.make_async_copy(ckv_hbm.at[pg], cbuf.at[slot, p], sem_c.at[slot]),
                pltpu.make_async_copy(kr_hbm.at[pg], rbuf.at[slot, p], sem_r.at[slot]))

    def issue(step, slot):
        for p in range(pages_per_step):
            cc, cr = copies(step, slot, p)
            cc.start()
            cr.start()

    @pl.when(s == 0)
    def _():
        issue(0, 0)

    @pl.when(s + 1 < n_steps)
    def _():
        issue(s + 1, (s + 1) % 2)

    slot = s % 2
    for p in range(pages_per_step):
        cc, cr = copies(s, slot, p)
        cc.wait()
        cr.wait()

    @pl.when(j == 0)
    def _():
        m_scr[...] = jnp.full_like(m_scr, -jnp.inf)
        l_scr[...] = jnp.zeros_like(l_scr)
        acc_scr[...] = jnp.zeros_like(acc_scr)

    q = q_ref[...]
    q_lat = q[:, :kvl]
    q_pe = q[:, kvl:kvl + QK_ROPE]
    nt = (((1,), (1,)), ((), ()))

    def update(c_b, r_b, mask):
        sc = (lax.dot_general(q_lat, c_b, nt, preferred_element_type=F32)
              + lax.dot_general(q_pe, r_b, nt, preferred_element_type=F32))
        if mask is not None:
            sc = jnp.where(mask, sc, -jnp.inf)
        m_old = m_scr[...]
        m_new = jnp.maximum(m_old, jnp.max(sc, axis=1, keepdims=True))
        alpha = jnp.exp2((m_old - m_new) * SCALE_LOG2E)
        pr = jnp.exp2((sc - m_new) * SCALE_LOG2E)
        l_scr[...] = alpha * l_scr[...] + jnp.sum(pr, axis=1, keepdims=True)
        acc_scr[...] = alpha * acc_scr[...] + jnp.dot(pr.astype(BF16), c_b, preferred_element_type=F32)
        m_scr[...] = m_new

    c_b = cbuf[slot].reshape(pages_per_step * page, kvl).astype(BF16)
    r_b = rbuf[slot].reshape(pages_per_step * page, QK_ROPE).astype(BF16)
    update(c_b, r_b, None)

    @pl.when(j == n_chunks - 1)
    def _():
        rows = q.shape[0]
        n_new = cnew_ref.shape[0]
        q_l = lax.broadcasted_iota(jnp.int32, (rows, n_new), 0) % dec_seq
        kk = lax.broadcasted_iota(jnp.int32, (rows, n_new), 1)
        update(cnew_ref[...].astype(BF16), rnew_ref[...].astype(BF16), kk <= q_l)
        o_ref[...] = acc_scr[...] / l_scr[...]


def _paged_attention(qcat, c_new, kr_new, cache_ckv, cache_krope, page_table, *, dec_seq, n_heads,
                     pages_per_step):
    n_dec, n_pages = page_table.shape
    page, kvl = cache_ckv.shape[1:]
    rows = n_heads * dec_seq
    width = qcat.shape[1]
    n_chunks = n_pages // pages_per_step
    n_new = c_new.shape[1]
    kern = functools.partial(_paged_kernel, n_chunks=n_chunks, pages_per_step=pages_per_step,
                             n_pages=n_pages, dec_seq=dec_seq, kvl=kvl)
    grid_spec = pltpu.PrefetchScalarGridSpec(
        num_scalar_prefetch=1,
        grid=(n_dec * n_chunks,),
        in_specs=[
            pl.BlockSpec((rows, width), lambda s, pt: (s // n_chunks, 0)),
            pl.BlockSpec((None, n_new, kvl), lambda s, pt: (s // n_chunks, 0, 0)),
            pl.BlockSpec((None, n_new, QK_ROPE), lambda s, pt: (s // n_chunks, 0, 0)),
            pl.BlockSpec(memory_space=pl.ANY),
            pl.BlockSpec(memory_space=pl.ANY),
        ],
        out_specs=pl.BlockSpec((rows, kvl), lambda s, pt: (s // n_chunks, 0)),
        scratch_shapes=[
            pltpu.VMEM((2, pages_per_step, page, kvl), F32),
            pltpu.VMEM((2, pages_per_step, page, QK_ROPE), F32),
            pltpu.SemaphoreType.DMA((2,)),
            pltpu.SemaphoreType.DMA((2,)),
            pltpu.VMEM((rows, 1), F32),
            pltpu.VMEM((rows, 1), F32),
            pltpu.VMEM((rows, kvl), F32),
        ],
    )
    return pl.pallas_call(
        kern,
        grid_spec=grid_spec,
        out_shape=jax.ShapeDtypeStruct((n_dec * rows, kvl), F32),
        compiler_params=_params(("arbitrary",)),
        name="paged_attention",
    )(page_table.reshape(-1), qcat, c_new, kr_new, cache_ckv, cache_krope)


def _ov_kernel(o_ref, w_ref, out_ref):
    out_ref[...] = jnp.dot(o_ref[...].astype(BF16), w_ref[...], preferred_element_type=F32)


def _value_up(o_lat_hm, w_uv_hm):
    n_heads, t, kvl = o_lat_hm.shape
    vh = w_uv_hm.shape[2]
    return pl.pallas_call(
        _ov_kernel,
        grid=(n_heads,),
        in_specs=[
            pl.BlockSpec((None, t, kvl), lambda h: (h, 0, 0)),
            pl.BlockSpec((None, kvl, vh), lambda h: (h, 0, 0)),
        ],
        out_specs=pl.BlockSpec((None, t, vh), lambda h: (h, 0, 0)),
        out_shape=jax.ShapeDtypeStruct((n_heads, t, vh), F32),
        compiler_params=_params(("arbitrary",)),
        name="value_up",
    )(o_lat_hm, w_uv_hm)


def _s5_weights(lam_re, lam_im, log_dt, b_re, b_im, c_re, c_im):
    lr = lam_re.astype(F32)
    li = lam_im.astype(F32)
    dt = jnp.exp(log_dt.astype(F32))[:, None]
    mag = jnp.exp(lr * dt)
    ang = li * dt
    ab_re = mag * jnp.cos(ang)
    ab_im = mag * jnp.sin(ang)
    den = lr * lr + li * li
    f_re = ((ab_re - 1.0) * lr + ab_im * li) / den
    f_im = (ab_im * lr - (ab_re - 1.0) * li) / den
    br = b_re.astype(F32)
    bi = b_im.astype(F32)
    bb_re = f_re[..., None] * br - f_im[..., None] * bi
    bb_im = f_re[..., None] * bi + f_im[..., None] * br
    g = lr.shape[0]
    gpc = S5_GROUPS_PER_CHUNK
    n_chunks = g // gpc
    eye = jnp.eye(gpc, dtype=F32)

    def b_block(bb):
        t = bb.reshape(n_chunks, gpc, SSM_STATE, SSM_GROUP).transpose(0, 1, 3, 2)
        blk = t[:, :, :, None, :] * eye[None, :, None, :, None]
        return blk.reshape(n_chunks, gpc * SSM_GROUP, gpc * SSM_STATE)

    def c_block(cc):
        t = cc.reshape(n_chunks, gpc, SSM_GROUP, SSM_STATE).transpose(0, 1, 3, 2)
        blk = t[:, :, :, None, :] * eye[None, :, None, :, None]
        return blk.reshape(n_chunks, gpc * SSM_STATE, gpc * SSM_GROUP)

    bblk = jnp.concatenate([b_block(bb_re), b_block(bb_im)], axis=2).astype(BF16)
    cblk = jnp.concatenate([c_block(c_re.astype(F32)), -c_block(c_im.astype(F32))], axis=1).astype(BF16)
    a_re = ab_re.reshape(n_chunks, 1, gpc * SSM_STATE)
    a_im = ab_im.reshape(n_chunks, 1, gpc * SSM_STATE)
    return bblk, cblk, a_re, a_im


def _rope_tables(pos, n_rep):
    inv = 1.0 / (ROPE_THETA ** (jnp.arange(0, QK_ROPE, 2, dtype=F32) / QK_ROPE))
    ang = pos.astype(F32)[:, None] * inv[None, :]
    cos = jnp.cos(ang)
    sin = jnp.sin(ang)
    length = pos.shape[0]
    zero_lo = jnp.zeros((length, QK_NOPE), F32)
    zero_hi = jnp.zeros((length, HEAD_PAD - QK_NOPE - QK_ROPE), F32)
    cos_k = jnp.concatenate([zero_lo, cos, cos, zero_hi], axis=1)
    sin_k = jnp.concatenate([zero_lo, sin, sin, zero_hi], axis=1)
    cos_q = jnp.concatenate([jnp.ones((length, QK_NOPE), F32), cos, cos, zero_hi], axis=1)
    if n_rep > 1:
        cos_k, sin_k, cos_q = (jnp.tile(a, (n_rep, 1)) for a in (cos_k, sin_k, cos_q))
    return cos_k, sin_k, cos_q


def _pad_heads(w, n_heads, width, offset=0):
    k = w.shape[0]
    w3 = w.reshape(k, n_heads, width)
    out = jnp.zeros((k, n_heads, HEAD_PAD), w.dtype).at[:, :, offset:offset + width].set(w3)
    return out.reshape(k, n_heads * HEAD_PAD)


def _swap_rope(w_rope):
    half = QK_ROPE // 2
    return jnp.concatenate([-w_rope[..., half:], w_rope[..., :half]], axis=-1)


def _mla_weights(p, b):
    n_heads = p["mla_w_uk"].shape[1]
    kvl = p["mla_w_uk"].shape[0]
    w_qb = p["mla_w_q_b"][b]
    ql = w_qb.shape[0]
    w3 = w_qb.reshape(ql, n_heads, QK_NOPE + QK_ROPE)
    zero_tail = jnp.zeros((ql, n_heads, HEAD_PAD - QK_NOPE - QK_ROPE), F32)
    wqb_p = jnp.concatenate([w3, zero_tail], axis=2).reshape(ql, n_heads * HEAD_PAD)
    wqb_s = jnp.concatenate([jnp.zeros((ql, n_heads, QK_NOPE), F32), _swap_rope(w3[..., QK_NOPE:]),
                             zero_tail], axis=2).reshape(ql, n_heads * HEAD_PAD)
    w_o = p["mla_w_o"][b]
    d_model = w_o.shape[1]
    w_o_p = jnp.zeros((n_heads, HEAD_PAD, d_model), F32).at[:, :V_HEAD].set(
        w_o.reshape(n_heads, V_HEAD, d_model)).reshape(n_heads * HEAD_PAD, d_model)
    del kvl
    return dict(wqa=p["mla_w_q_a"][b].astype(BF16), g_q=p["mla_g_q"][b][None],
                wqb_p=wqb_p.astype(BF16), wqb_s=wqb_s.astype(BF16),
                w_o=w_o.astype(BF16), w_o_p=w_o_p.astype(BF16))


def _shared_mla_weights(p):
    w_kv_a = p["mla_w_kv_a"]
    d_model = w_kv_a.shape[0]
    kvl, n_heads, _ = p["mla_w_uk"].shape
    w_rope = w_kv_a[:, kvl:]
    pad_lo = jnp.zeros((d_model, QK_NOPE), F32)
    pad_hi = jnp.zeros((d_model, HEAD_PAD - QK_NOPE - QK_ROPE), F32)
    wr = jnp.concatenate([pad_lo, w_rope, pad_hi], axis=1)
    wrs = jnp.concatenate([pad_lo, _swap_rope(w_rope), pad_hi], axis=1)
    wuk_p = _pad_heads(p["mla_w_uk"].reshape(kvl, n_heads * QK_NOPE), n_heads, QK_NOPE)
    wuv_p = _pad_heads(p["mla_w_uv"].reshape(kvl, n_heads * V_HEAD), n_heads, V_HEAD)
    width = kvl + LANES
    m_heads = jnp.zeros((n_heads, HEAD_PAD, width), F32)
    m_heads = m_heads.at[:, :QK_NOPE, :kvl].set(p["mla_w_uk"].transpose(1, 2, 0))
    m_heads = m_heads.at[:, QK_NOPE:QK_NOPE + QK_ROPE, kvl:kvl + QK_ROPE].set(
        jnp.broadcast_to(jnp.eye(QK_ROPE, dtype=F32), (n_heads, QK_ROPE, QK_ROPE)))
    return dict(wc=w_kv_a[:, :kvl].astype(BF16), g_kv=p["mla_g_kv"][None], wr=wr.astype(BF16),
                wrs=wrs.astype(BF16), wuk_p=wuk_p.astype(BF16), wuv_p=wuv_p.astype(BF16),
                m_heads=m_heads.astype(BF16), w_uv_hm=p["mla_w_uv"].transpose(1, 0, 2).astype(BF16))


def _tiles(t):
    big = t >= 4096
    return dict(tm=256 if big else min(t, 128), bm=512 if big else 64,
                tt_route=512 if big else min(t, 128), tt_scatter=256 if big else min(t, 128),
                tt_combine=128 if big else min(t, 128))


def _trunk(x, pos, h0, past, p, s5w, mla_shared, mla_layers):
    n_seq, seq, d_model = x.shape
    t = n_seq * seq
    cfg = _tiles(t)
    n_a = len(s5w)
    n_states = d_model // SSM_GROUP * SSM_STATE

    assert d_model == ROW_TILE * LANES, "MoE row buffers hold one (8, 128) tile per row"

    xt = x.transpose(1, 0, 2).reshape(t, d_model)
    lt = min(seq, 256)
    new_re, new_im = [], []
    for a in range(n_a):
        bblk, cblk, a_re, a_im = s5w[a]
        if h0 is None:
            h0r = jnp.zeros((n_seq, n_states), F32)
            h0i = jnp.zeros((n_seq, n_states), F32)
        else:
            h0r = h0[0][a].reshape(n_seq, n_states)
            h0i = h0[1][a].reshape(n_seq, n_states)
        z, hr, hi = _s5_scan(xt, bblk, cblk, a_re, a_im, p["ssm_d"][a][None], h0r, h0i,
                             n_seq=n_seq, lt=lt)
        new_re.append(hr.reshape(n_seq, d_model // SSM_GROUP, SSM_STATE))
        new_im.append(hi.reshape(n_seq, d_model // SSM_GROUP, SSM_STATE))
        proj = _proj_ln_router(z, p["ssm_w_glu_bf"][a], p["ssm_b_glu"][a][None], xt,
                               p["ln_mix_g"][a][None], p["ln_mix_b"][a][None],
                               p["moe_w_router"][a], p["moe_b_router"][a][None],
                               glu=True, tm=cfg["tm"])
        xt = yield proj

    xb = xt.reshape(seq, n_seq, d_model).transpose(1, 0, 2).reshape(t, d_model)
    tm = cfg["tm"]
    n_rep = 1 if seq >= tm else tm // seq
    cos_k, sin_k, cos_q = _rope_tables(pos, n_rep)
    prompt = past is None
    lat = _mla_latent(xb, mla_shared["wc"], mla_shared["g_kv"], mla_shared["wr"], mla_shared["wrs"],
                      cos_k, sin_k, mla_shared["wuk_p"], mla_shared["wuv_p"], tm=tm, with_kv=prompt)
    c_lat, kr128 = lat[0], lat[1]
    kr = kr128[:, QK_NOPE:QK_NOPE + QK_ROPE]
    kvl = c_lat.shape[1]
    n_heads = mla_shared["m_heads"].shape[0]
    zero_bias = jnp.zeros((1, d_model), F32)
    if not prompt:
        cache_ckv, cache_krope, page_table = past
        n_new = -(-seq // SUBLANES) * SUBLANES
        c_new = jnp.zeros((n_seq, n_new, kvl), F32).at[:, :seq].set(c_lat.reshape(n_seq, seq, kvl))
        kr_new = jnp.zeros((n_seq, n_new, QK_ROPE), F32).at[:, :seq].set(kr.reshape(n_seq, seq, QK_ROPE))
    for b, mw in enumerate(mla_layers):
        layer = n_a + b
        qp = _mla_queries(xb, mw["wqa"], mw["g_q"], mw["wqb_p"], mw["wqb_s"], cos_q, sin_k, tm=tm)
        if prompt:
            o = _flash_attention(qp, lat[2], lat[3], n_seq=n_seq, tq=min(seq, 512))
            w_o = mw["w_o_p"]
        else:
            qcat = _q_latent(qp, mla_shared["m_heads"])
            width = qcat.shape[2]
            qcat = qcat.reshape(n_heads, n_seq, seq, width).transpose(1, 0, 2, 3).reshape(
                n_seq * n_heads * seq, width)
            o_lat = _paged_attention(qcat, c_new, kr_new, cache_ckv, cache_krope, page_table,
                                     dec_seq=seq, n_heads=n_heads,
                                     pages_per_step=math.gcd(page_table.shape[1], PAGES_PER_STEP))
            o_hm = o_lat.reshape(n_seq, n_heads, seq, kvl).transpose(1, 0, 2, 3).reshape(n_heads, t, kvl)
            o = _value_up(o_hm, mla_shared["w_uv_hm"])
            o = o.transpose(1, 0, 2).reshape(t, n_heads * V_HEAD)
            w_o = mw["w_o"]
        proj = _proj_ln_router(o, w_o, zero_bias, xb, p["ln_mix_g"][layer][None],
                               p["ln_mix_b"][layer][None], p["moe_w_router"][layer],
                               p["moe_b_router"][layer][None], glu=False, tm=tm)
        xb = yield proj
    y = xb.reshape(n_seq, seq, d_model)
    return (y, jnp.stack(new_re), jnp.stack(new_im), c_lat.reshape(n_seq, seq, kvl),
            kr.reshape(n_seq, seq, QK_ROPE))


def kernel(x_prompt, x_sample, state_ssm_re, state_ssm_im, cache_ckv, cache_krope, page_table,
           ssm_lam_re, ssm_lam_im, ssm_log_dt, ssm_b_re, ssm_b_im, ssm_c_re, ssm_c_im, ssm_d,
           ssm_w_glu, ssm_b_glu, mla_w_kv_a, mla_g_kv, mla_w_uk, mla_w_uv, mla_w_q_a, mla_g_q,
           mla_w_q_b, mla_w_o, moe_w_router, moe_b_router, moe_w_gu, moe_b_gu, moe_w_dn, moe_b_dn,
           ln_mix_g, ln_mix_b, ln_ffn_g, ln_ffn_b):
    p = dict(ssm_d=ssm_d, ssm_b_glu=ssm_b_glu, ssm_w_glu_bf=ssm_w_glu.astype(BF16),
             mla_w_kv_a=mla_w_kv_a, mla_g_kv=mla_g_kv, mla_w_uk=mla_w_uk, mla_w_uv=mla_w_uv,
             mla_w_q_a=mla_w_q_a, mla_g_q=mla_g_q, mla_w_q_b=mla_w_q_b, mla_w_o=mla_w_o,
             moe_w_router=moe_w_router, moe_b_router=moe_b_router, moe_w_gu=moe_w_gu,
             moe_b_gu4=moe_b_gu[:, :, None, :], moe_w_dn=moe_w_dn, moe_b_dn4=moe_b_dn[:, :, None, :],
             ln_mix_g=ln_mix_g, ln_mix_b=ln_mix_b, ln_ffn_g=ln_ffn_g, ln_ffn_b=ln_ffn_b)
    n_a = ssm_lam_re.shape[0]
    s5w = [_s5_weights(ssm_lam_re[a], ssm_lam_im[a], ssm_log_dt[a], ssm_b_re[a], ssm_b_im[a],
                       ssm_c_re[a], ssm_c_im[a]) for a in range(n_a)]
    mla_shared = _shared_mla_weights(p)
    mla_layers = [_mla_weights(p, b) for b in range(mla_w_q_a.shape[0])]

    n_pages = page_table.shape[1]
    past_len = n_pages * cache_ckv.shape[1]
    pos_sample = past_len + jnp.arange(x_sample.shape[1])
    stream_s = _trunk(x_sample, pos_sample, (state_ssm_re, state_ssm_im),
                      (cache_ckv, cache_krope, page_table), p, s5w, mla_shared, mla_layers)
    pos_prompt = jnp.arange(x_prompt.shape[1])
    stream_p = _trunk(x_prompt, pos_prompt, None, None, p, s5w, mla_shared, mla_layers)
    cfg = _tiles(x_prompt.shape[0] * x_prompt.shape[1])
    tt_small = min(x_sample.shape[0] * x_sample.shape[1], 128)
    req_s, req_p = next(stream_s), next(stream_p)
    out_s = out_p = None
    for layer in range(moe_w_gu.shape[0]):
        new_p, new_s = _moe_pair(req_p, req_s, p, layer, bm=cfg["bm"], tt_route=cfg["tt_route"],
                                 tt_scatter=cfg["tt_scatter"], tt_combine=cfg["tt_combine"],
                                 tt_small=tt_small)
        try:
            req_s = stream_s.send(new_s)
        except StopIteration as done:
            out_s = done.value
        try:
            req_p = stream_p.send(new_p)
        except StopIteration as done:
            out_p = done.value
    return (out_p[0], out_s[0], out_p[1], out_p[2], out_p[3], out_p[4],
            out_s[1], out_s[2], out_s[3], out_s[4])
```

```python
import functools
import math

import jax
import jax.numpy as jnp
from jax import lax
from jax.experimental import pallas as pl
from jax.experimental.pallas import tpu as pltpu

F32 = jnp.float32
BF16 = jnp.bfloat16

SSM_GROUP = 16
SSM_STATE = 64
QK_NOPE = 64
QK_ROPE = 32
V_HEAD = 64
ROPE_THETA = 10000.0
ATTN_SCALE = (QK_NOPE + QK_ROPE) ** -0.5
SCALE_LOG2E = ATTN_SCALE * math.log2(math.e)
TOP_K = 4
SWIGLU_LIMIT = 7.0
SWIGLU_ALPHA = 1.702
LN_EPS = 1e-5
RMS_EPS = 1e-6
DEPTH = 4
DEEPNORM_ALPHA = (2 * DEPTH) ** 0.25

LANES = 128
SUBLANES = 8
ROW_TILE = SUBLANES
HEAD_PAD = LANES
PAGES_PER_STEP = 64
DENSE_EXPERTS_PER_STEP = 8
ZERO_ROWS = 64
DMA_UNROLL = 8
VMEM_LIMIT = 56 * 1024 * 1024

S5_CHUNK = LANES
S5_GROUPS_PER_CHUNK = S5_CHUNK // SSM_GROUP
S5_STATES_PER_CHUNK = S5_GROUPS_PER_CHUNK * SSM_STATE


def _params(sem, vmem=VMEM_LIMIT):
    return pltpu.CompilerParams(dimension_semantics=sem, vmem_limit_bytes=vmem)


def _layer_norm(x, g, b):
    mu = jnp.mean(x, axis=-1, keepdims=True)
    xc = x - mu
    var = jnp.mean(xc * xc, axis=-1, keepdims=True)
    return xc * lax.rsqrt(var + LN_EPS) * g + b


def _rms_norm(x, g):
    return x * lax.rsqrt(jnp.mean(x * x, axis=-1, keepdims=True) + RMS_EPS) * g


def _gelu_tanh(x):
    c = math.sqrt(2.0 / math.pi)
    return 0.5 * x * (1.0 + jnp.tanh(c * (x + 0.044715 * (x * x * x))))


def _s5_kernel(x_ref, bblk_ref, cblk_ref, are_ref, aim_ref, d_ref, h0r_ref, h0i_ref,
               z_ref, hr_out, hi_out, hbuf, st_re, st_im, *, n_seq, lt):
    i = pl.program_id(1)
    ns = S5_STATES_PER_CHUNK

    @pl.when(i == 0)
    def _():
        st_re[...] = h0r_ref[...]
        st_im[...] = h0i_ref[...]

    u = x_ref[...]
    hbuf[...] = jnp.dot(u.astype(BF16), bblk_ref[...], preferred_element_type=F32)
    a_re = jnp.broadcast_to(are_ref[...], (n_seq, ns))
    a_im = jnp.broadcast_to(aim_ref[...], (n_seq, ns))

    def step(l, carry):
        h_re, h_im = carry
        r0 = pl.multiple_of(l * n_seq, n_seq)
        bu_re = hbuf[pl.ds(r0, n_seq), 0:ns]
        bu_im = hbuf[pl.ds(r0, n_seq), ns:2 * ns]
        n_re = a_re * h_re - a_im * h_im + bu_re
        n_im = a_re * h_im + a_im * h_re + bu_im
        hbuf[pl.ds(r0, n_seq), 0:ns] = n_re
        hbuf[pl.ds(r0, n_seq), ns:2 * ns] = n_im
        return n_re, n_im

    h_re, h_im = lax.fori_loop(0, lt, step, (st_re[...], st_im[...]), unroll=min(lt, 8))
    st_re[...] = h_re
    st_im[...] = h_im

    y = jnp.dot(hbuf[...].astype(BF16), cblk_ref[...], preferred_element_type=F32)
    z_ref[...] = _gelu_tanh(y + d_ref[...] * u)

    @pl.when(i == pl.num_programs(1) - 1)
    def _():
        hr_out[...] = h_re
        hi_out[...] = h_im


def _s5_scan(x_tm, bblk, cblk, a_re, a_im, d_skip, h0_re, h0_im, *, n_seq, lt):
    rows, d_model = x_tm.shape
    n_chunks = d_model // S5_CHUNK
    ns = S5_STATES_PER_CHUNK
    seq = rows // n_seq
    tile = lt * n_seq
    kern = functools.partial(_s5_kernel, n_seq=n_seq, lt=lt)
    return pl.pallas_call(
        kern,
        grid=(n_chunks, seq // lt),
        in_specs=[
            pl.BlockSpec((tile, S5_CHUNK), lambda c, i: (i, c)),
            pl.BlockSpec((None, S5_CHUNK, 2 * ns), lambda c, i: (c, 0, 0)),
            pl.BlockSpec((None, 2 * ns, S5_CHUNK), lambda c, i: (c, 0, 0)),
            pl.BlockSpec((None, 1, ns), lambda c, i: (c, 0, 0)),
            pl.BlockSpec((None, 1, ns), lambda c, i: (c, 0, 0)),
            pl.BlockSpec((1, S5_CHUNK), lambda c, i: (0, c)),
            pl.BlockSpec((n_seq, ns), lambda c, i: (0, c)),
            pl.BlockSpec((n_seq, ns), lambda c, i: (0, c)),
        ],
        out_specs=[
            pl.BlockSpec((tile, S5_CHUNK), lambda c, i: (i, c)),
            pl.BlockSpec((n_seq, ns), lambda c, i: (0, c)),
            pl.BlockSpec((n_seq, ns), lambda c, i: (0, c)),
        ],
        out_shape=[
            jax.ShapeDtypeStruct((rows, d_model), F32),
            jax.ShapeDtypeStruct((n_seq, n_chunks * ns), F32),
            jax.ShapeDtypeStruct((n_seq, n_chunks * ns), F32),
        ],
        scratch_shapes=[
            pltpu.VMEM((tile, 2 * ns), F32),
            pltpu.VMEM((n_seq, ns), F32),
            pltpu.VMEM((n_seq, ns), F32),
        ],
        compiler_params=_params(("arbitrary", "arbitrary")),
        name="s5_scan",
    )(x_tm, bblk, cblk, a_re, a_im, d_skip, h0_re, h0_im)


def _split_bf16(v):
    hi = v.astype(BF16)
    return hi, (v - hi.astype(F32)).astype(BF16)


def _proj_ln_router_kernel(a_ref, w_ref, b_ref, x_ref, g_ref, beta_ref, wrh_ref, wrl_ref, br_ref,
                           x1_ref, xrow_ref, lg_ref, *, glu):
    h = jnp.dot(a_ref[...].astype(BF16), w_ref[...], preferred_element_type=F32) + b_ref[...]
    if glu:
        d = h.shape[1] // 2
        mix = h[:, :d] * jax.nn.sigmoid(h[:, d:])
    else:
        mix = h
    x1 = _layer_norm(DEEPNORM_ALPHA * x_ref[...] + mix, g_ref[...], beta_ref[...])
    x1_ref[...] = x1
    tm, d_model = x1.shape
    for c in range(d_model // LANES):
        xrow_ref[pl.ds(c, tm, stride=ROW_TILE), :] = x1[:, c * LANES:(c + 1) * LANES]
    xh, xl = _split_bf16(x1)
    lg_ref[...] = (jnp.dot(xh, wrh_ref[...], preferred_element_type=F32)
                   + jnp.dot(xl, wrh_ref[...], preferred_element_type=F32)
                   + jnp.dot(xh, wrl_ref[...], preferred_element_type=F32)) + br_ref[...]


def _proj_ln_router(a, w, b, x, g, beta, wr, br, *, glu, tm):
    t, k = a.shape
    nw = w.shape[1]
    d = x.shape[1]
    ne = wr.shape[1]
    wrh, wrl = _split_bf16(wr.astype(F32))
    kern = functools.partial(_proj_ln_router_kernel, glu=glu)
    return pl.pallas_call(
        kern,
        grid=(t // tm,),
        in_specs=[
            pl.BlockSpec((tm, k), lambda i: (i, 0)),
            pl.BlockSpec((k, nw), lambda i: (0, 0)),
            pl.BlockSpec((1, nw), lambda i: (0, 0)),
            pl.BlockSpec((tm, d), lambda i: (i, 0)),
            pl.BlockSpec((1, d), lambda i: (0, 0)),
            pl.BlockSpec((1, d), lambda i: (0, 0)),
            pl.BlockSpec((d, ne), lambda i: (0, 0)),
            pl.BlockSpec((d, ne), lambda i: (0, 0)),
            pl.BlockSpec((1, ne), lambda i: (0, 0)),
        ],
        out_specs=[
            pl.BlockSpec((tm, d), lambda i: (i, 0)),
            pl.BlockSpec((tm * ROW_TILE, LANES), lambda i: (i, 0)),
            pl.BlockSpec((tm, ne), lambda i: (i, 0)),
        ],
        out_shape=[
            jax.ShapeDtypeStruct((t, d), F32),
            jax.ShapeDtypeStruct((t * ROW_TILE, LANES), F32),
            jax.ShapeDtypeStruct((t, ne), F32),
        ],
        compiler_params=_params(("arbitrary",)),
        name="proj_ln_router",
    )(a, w, b, x, g, beta, wrh, wrl, br)


def _route_kernel(lg_ref, tri_ref, eidx_ref, rank_ref, gate_ref, cnt_ref, base_scr):
    i = pl.program_id(0)

    @pl.when(i == 0)
    def _():
        base_scr[...] = jnp.zeros_like(base_scr)

    l = lg_ref[...]
    tt, ne = l.shape
    lane = lax.broadcasted_iota(jnp.int32, (tt, ne), 1).astype(F32)
    tops, idxs, hots = [], [], []
    for _ in range(TOP_K):
        m = jnp.max(l, axis=1, keepdims=True)
        idx = jnp.min(jnp.where(l == m, lane, float(ne)), axis=1, keepdims=True)
        hot = lane == idx
        tops.append(m)
        idxs.append(idx.astype(jnp.int32))
        hots.append(hot)
        l = jnp.where(hot, -jnp.inf, l)
    sel = jnp.zeros((tt, ne), F32)
    for hot in hots:
        sel = sel + jnp.where(hot, 1.0, 0.0)
    prefix = jnp.dot(tri_ref[...], sel.astype(BF16), preferred_element_type=F32) + base_scr[...]
    es = [jnp.exp(m - tops[0]) for m in tops]
    den = es[0] + es[1] + es[2] + es[3]
    wide = lax.broadcasted_iota(jnp.int32, (tt, LANES), 1)
    eidx_w = jnp.zeros((tt, LANES), jnp.int32)
    rank_w = jnp.zeros((tt, LANES), jnp.int32)
    gate_w = jnp.zeros((tt, LANES), F32)
    for k in range(TOP_K):
        rank_k = jnp.sum(jnp.where(hots[k], prefix, 0.0), axis=1, keepdims=True).astype(jnp.int32)
        eidx_w = jnp.where(wide == k, idxs[k], eidx_w)
        rank_w = jnp.where(wide == k, rank_k, rank_w)
        gate_w = jnp.where(wide == k, es[k] / den, gate_w)
    eidx_ref[...] = eidx_w
    rank_ref[...] = rank_w
    gate_ref[...] = gate_w
    base_scr[...] = base_scr[...] + jnp.sum(sel, axis=0, keepdims=True)
    cnt_ref[...] = base_scr[...].astype(jnp.int32)


def _route(logits, *, tt):
    t, ne = logits.shape
    tri = jnp.tri(tt, k=-1, dtype=BF16)
    return pl.pallas_call(
        _route_kernel,
        grid=(t // tt,),
        in_specs=[
            pl.BlockSpec((tt, ne), lambda i: (i, 0)),
            pl.BlockSpec((tt, tt), lambda i: (0, 0)),
        ],
        out_specs=[
            pl.BlockSpec((tt, LANES), lambda i: (i, 0)),
            pl.BlockSpec((tt, LANES), lambda i: (i, 0)),
            pl.BlockSpec((tt, LANES), lambda i: (i, 0)),
            pl.BlockSpec((1, ne), lambda i: (0, 0)),
        ],
        out_shape=[
            jax.ShapeDtypeStruct((t, LANES), jnp.int32),
            jax.ShapeDtypeStruct((t, LANES), jnp.int32),
            jax.ShapeDtypeStruct((t, LANES), F32),
            jax.ShapeDtypeStruct((1, ne), jnp.int32),
        ],
        scratch_shapes=[pltpu.VMEM((1, ne), F32)],
        compiler_params=_params(("arbitrary",)),
        name="moe_route",
    )(logits, tri)


def _tile_rows(row):
    return pl.ds(pl.multiple_of(row * ROW_TILE, ROW_TILE), ROW_TILE)


def _scatter_kernel(dest_ref, pad_ref, x_ref, rows_out, zero_rows, sem, zsem, zsem_chunk):
    i = pl.program_id(0)
    n_assign = dest_ref.shape[1]
    tt = n_assign // TOP_K

    def row_copy(src_row, dst_row):
        return pltpu.make_async_copy(x_ref.at[_tile_rows(src_row)], rows_out.at[_tile_rows(dst_row)], sem)

    def drain(j, carry):
        row_copy(0, 0).wait()
        return carry

    for k in range(TOP_K):
        def issue(g, carry, k=k):
            for u in range(DMA_UNROLL):
                t = g * DMA_UNROLL + u
                row_copy(t, dest_ref[0, k * tt + t]).start(priority=u % 2)
            return carry
        lax.fori_loop(0, tt // DMA_UNROLL, issue, 0)

    @pl.when(i == pl.num_programs(0) - 1)
    def _():
        zero_rows[...] = jnp.zeros_like(zero_rows)

        def zero_copy(dst_row):
            return pltpu.make_async_copy(zero_rows.at[pl.ds(0, ROW_TILE)],
                                         rows_out.at[_tile_rows(dst_row)], zsem)

        def zero_chunk_copy(dst_row):
            dst = pl.ds(pl.multiple_of(dst_row * ROW_TILE, ROW_TILE), ZERO_ROWS * ROW_TILE)
            return pltpu.make_async_copy(zero_rows, rows_out.at[dst], zsem_chunk)

        def per_range(r, counts):
            n_single, n_chunk = counts
            lo = pad_ref[0, r]
            hi = pad_ref[1, r]
            chunks = (hi - lo) // ZERO_ROWS

            def chunk_body(c, carry):
                zero_chunk_copy(lo + c * ZERO_ROWS).start()
                return carry

            def single_body(row, carry):
                zero_copy(row).start()
                return carry

            lax.fori_loop(0, chunks, chunk_body, 0)
            lax.fori_loop(lo + chunks * ZERO_ROWS, hi, single_body, 0)
            return n_single + (hi - lo) - chunks * ZERO_ROWS, n_chunk + chunks

        n_single, n_chunk = lax.fori_loop(0, pad_ref.shape[1], per_range, (0, 0))

        def chunk_drain(c, carry):
            zero_chunk_copy(0).wait()
            return carry

        def single_drain(r, carry):
            zero_copy(0).wait()
            return carry

        lax.fori_loop(0, n_chunk, chunk_drain, 0)
        lax.fori_loop(0, n_single, single_drain, 0)

    lax.fori_loop(0, n_assign, drain, 0, unroll=DMA_UNROLL)


def _k_major_tiles(dest, tt):
    t = dest.shape[0]
    return dest.reshape(t // tt, tt, TOP_K).transpose(0, 2, 1).reshape(t // tt, 1, TOP_K * tt)


def _moe_scatter(x_rows_src, dest, pad_ranges, n_rows, *, tt):
    t = x_rows_src.shape[0] // ROW_TILE
    dest2 = _k_major_tiles(dest, tt)
    return pl.pallas_call(
        _scatter_kernel,
        grid=(t // tt,),
        in_specs=[
            pl.BlockSpec((None, 1, tt * TOP_K), lambda i: (i, 0, 0), memory_space=pltpu.SMEM),
            pl.BlockSpec(pad_ranges.shape, lambda i: (0, 0), memory_space=pltpu.SMEM),
            pl.BlockSpec((tt * ROW_TILE, LANES), lambda i: (i, 0)),
        ],
        out_specs=pl.BlockSpec(memory_space=pl.ANY),
        out_shape=jax.ShapeDtypeStruct((n_rows * ROW_TILE, LANES), F32),
        scratch_shapes=[pltpu.VMEM((ZERO_ROWS * ROW_TILE, LANES), F32), pltpu.SemaphoreType.DMA(()),
                        pltpu.SemaphoreType.DMA(()), pltpu.SemaphoreType.DMA(())],
        compiler_params=_params(("arbitrary",)),
        name="moe_scatter",
    )(dest2, pad_ranges, x_rows_src)


def _ffn_kernel(be_ref, nu_ref, x_ref, xs_ref, wgu_ref, bgu_ref, wdn_ref, bdn_ref, y_ref, ys_ref,
                wgu_bf, wdn_bf, x_bf):
    b = pl.program_id(0)
    prev = be_ref[jnp.maximum(b - 1, 0)]
    changed = jnp.logical_or(b == 0, be_ref[b] != prev)
    bm, d_model = x_bf.shape
    n_lane_tiles = d_model // LANES

    def expert_ffn(xb):
        de = wdn_bf.shape[0]
        h = jnp.dot(xb, wgu_bf[...], preferred_element_type=F32) + bgu_ref[...]
        g = jnp.minimum(h[:, :de], SWIGLU_LIMIT)
        up = jnp.clip(h[:, de:], -SWIGLU_LIMIT, SWIGLU_LIMIT)
        act = (up + 1.0) * (g * jax.nn.sigmoid(SWIGLU_ALPHA * g))
        return jnp.dot(act.astype(BF16), wdn_bf[...], preferred_element_type=F32) + bdn_ref[...]

    @pl.when(changed)
    def _():
        wgu_bf[...] = wgu_ref[...].astype(BF16)
        wdn_bf[...] = wdn_ref[...].astype(BF16)
        ys_ref[...] = expert_ffn(xs_ref[...])

    @pl.when(b < nu_ref[0])
    def _():
        for c in range(n_lane_tiles):
            x_bf[:, c * LANES:(c + 1) * LANES] = x_ref[pl.ds(c, bm, stride=ROW_TILE), :].astype(BF16)
        y = expert_ffn(x_bf[...])
        for c in range(n_lane_tiles):
            y_ref[pl.ds(c, bm, stride=ROW_TILE), :] = y[:, c * LANES:(c + 1) * LANES]

    @pl.when(b >= nu_ref[0])
    def _():
        y_ref[...] = jnp.zeros_like(y_ref)


def _moe_ffn(x_rows, xs_bf, block_e, n_used, w_gu, b_gu, w_dn, b_dn, layer, *, bm):
    n_rows = x_rows.shape[0] // ROW_TILE
    ts = xs_bf.shape[0]
    ne = w_dn.shape[1]
    d, de = w_dn.shape[3], w_dn.shape[2]
    grid_spec = pltpu.PrefetchScalarGridSpec(
        num_scalar_prefetch=2,
        grid=(n_rows // bm,),
        in_specs=[
            pl.BlockSpec((bm * ROW_TILE, LANES), lambda b, be, nu: (jnp.minimum(b, nu[0] - 1), 0)),
            pl.BlockSpec((ts, d), lambda b, be, nu: (0, 0)),
            pl.BlockSpec((None, None, d, 2 * de), lambda b, be, nu: (layer, be[b], 0, 0)),
            pl.BlockSpec((None, None, 1, 2 * de), lambda b, be, nu: (layer, be[b], 0, 0)),
            pl.BlockSpec((None, None, de, d), lambda b, be, nu: (layer, be[b], 0, 0)),
            pl.BlockSpec((None, None, 1, d), lambda b, be, nu: (layer, be[b], 0, 0)),
        ],
        out_specs=[
            pl.BlockSpec((bm * ROW_TILE, LANES), lambda b, be, nu: (b, 0)),
            pl.BlockSpec((None, ts, d), lambda b, be, nu: (be[b], 0, 0)),
        ],
        scratch_shapes=[pltpu.VMEM((d, 2 * de), BF16), pltpu.VMEM((de, d), BF16),
                        pltpu.VMEM((bm, d), BF16)],
    )
    return pl.pallas_call(
        _ffn_kernel,
        grid_spec=grid_spec,
        out_shape=[jax.ShapeDtypeStruct((n_rows * ROW_TILE, LANES), F32),
                   jax.ShapeDtypeStruct((ne, ts, d), F32)],
        compiler_params=_params(("arbitrary",)),
        name="moe_ffn",
    )(block_e, n_used, x_rows, xs_bf, w_gu, b_gu, w_dn, b_dn)


def _dense_combine_kernel(gate_ref, ys_ref, x_ref, g_ref, beta_ref, out_ref, acc):
    e = pl.program_id(0)

    @pl.when(e == 0)
    def _():
        acc[...] = jnp.zeros_like(acc)

    total = acc[...]
    for j in range(gate_ref.shape[0]):
        gate = gate_ref[j]
        total = total + jnp.where(gate != 0.0, gate * ys_ref[j], 0.0)
    acc[...] = total

    @pl.when(e == pl.num_programs(0) - 1)
    def _():
        out_ref[...] = _layer_norm(DEEPNORM_ALPHA * x_ref[...] + total, g_ref[...], beta_ref[...])


def _moe_dense_combine(ys, gate_dense, x1, g, beta):
    ne, ts, d = ys.shape
    per_step = math.gcd(ne, DENSE_EXPERTS_PER_STEP)
    return pl.pallas_call(
        _dense_combine_kernel,
        grid=(ne // per_step,),
        in_specs=[
            pl.BlockSpec((per_step, ts, 1), lambda e: (e, 0, 0)),
            pl.BlockSpec((per_step, ts, d), lambda e: (e, 0, 0)),
            pl.BlockSpec((ts, d), lambda e: (0, 0)),
            pl.BlockSpec((1, d), lambda e: (0, 0)),
            pl.BlockSpec((1, d), lambda e: (0, 0)),
        ],
        out_specs=pl.BlockSpec((ts, d), lambda e: (0, 0)),
        out_shape=jax.ShapeDtypeStruct((ts, d), F32),
        scratch_shapes=[pltpu.VMEM((ts, d), F32)],
        compiler_params=_params(("arbitrary",)),
        name="moe_dense_combine",
    )(gate_dense, ys, x1, g, beta)


def _combine_kernel(dcur_ref, dnxt_ref, gate_ref, x_ref, g_ref, beta_ref, yrows, out_ref, buf, sem):
    i = pl.program_id(0)
    n_steps = pl.num_programs(0)
    n_assign = dcur_ref.shape[1]
    tt = n_assign // TOP_K

    def row_copy(dref, k, t, slot):
        return pltpu.make_async_copy(yrows.at[_tile_rows(dref[0, k * tt + t])],
                                     buf.at[slot, k, _tile_rows(t)], sem.at[slot])

    def issue(dref, slot):
        for k in range(TOP_K):
            def body(g, carry, k=k):
                for u in range(DMA_UNROLL):
                    row_copy(dref, k, g * DMA_UNROLL + u, slot).start(priority=u % 2)
                return carry
            lax.fori_loop(0, tt // DMA_UNROLL, body, 0)

    @pl.when(i == 0)
    def _():
        issue(dcur_ref, 0)

    @pl.when(i + 1 < n_steps)
    def _():
        issue(dnxt_ref, (i + 1) % 2)

    slot = i % 2

    def drain(j, carry):
        row_copy(dcur_ref, 0, 0, slot).wait()
        return carry

    lax.fori_loop(0, n_assign, drain, 0, unroll=DMA_UNROLL)

    gates = gate_ref[...]
    d_model = x_ref.shape[1]
    pieces = []
    for c in range(d_model // LANES):
        f = gates[:, 0:1] * buf.at[slot, 0][pl.ds(c, tt, stride=ROW_TILE), :]
        for k in range(1, TOP_K):
            f = f + gates[:, k:k + 1] * buf.at[slot, k][pl.ds(c, tt, stride=ROW_TILE), :]
        pieces.append(DEEPNORM_ALPHA * x_ref[:, c * LANES:(c + 1) * LANES] + f)
    out_ref[...] = _layer_norm(jnp.concatenate(pieces, axis=1), g_ref[...], beta_ref[...])


def _moe_combine(y_rows, dest, gate_w, x1, g, beta, *, tt):
    t, d = x1.shape
    n_steps = t // tt
    dest2 = _k_major_tiles(dest, tt)
    return pl.pallas_call(
        _combine_kernel,
        grid=(n_steps,),
        in_specs=[
            pl.BlockSpec((None, 1, tt * TOP_K), lambda i: (i, 0, 0), memory_space=pltpu.SMEM),
            pl.BlockSpec((None, 1, tt * TOP_K), lambda i: (jnp.minimum(i + 1, n_steps - 1), 0, 0),
                         memory_space=pltpu.SMEM),
            pl.BlockSpec((tt, LANES), lambda i: (i, 0)),
            pl.BlockSpec((tt, d), lambda i: (i, 0)),
            pl.BlockSpec((1, d), lambda i: (0, 0)),
            pl.BlockSpec((1, d), lambda i: (0, 0)),
            pl.BlockSpec(memory_space=pl.ANY),
        ],
        out_specs=pl.BlockSpec((tt, d), lambda i: (i, 0)),
        out_shape=jax.ShapeDtypeStruct((t, d), F32),
        scratch_shapes=[pltpu.VMEM((2, TOP_K, tt * ROW_TILE, LANES), F32), pltpu.SemaphoreType.DMA((2,))],
        compiler_params=_params(("arbitrary",)),
        name="moe_combine",
    )(dest2, dest2, gate_w, x1, g, beta, y_rows)


def _moe_pair(big, small, p, layer, *, bm, tt_route, tt_scatter, tt_combine, tt_small):
    x1, x1_rows, logits = big
    xs1, _, logits_s = small
    t, d = x1.shape
    ne = logits.shape[1]
    eidx_w, rank_w, gate_w, cnt = _route(logits, tt=tt_route)
    eidx_s, _, gate_s, _ = _route(logits_s, tt=tt_small)
    experts = jnp.arange(ne, dtype=jnp.int32)
    gate_dense = jnp.sum(jnp.where(eidx_s[:, :TOP_K, None] == experts, gate_s[:, :TOP_K, None], 0.0), axis=1)
    gate_dense = gate_dense.T[:, :, None]
    cnt = cnt[0]
    padded = jnp.maximum((cnt + bm - 1) // bm, 1) * bm
    pend = jnp.cumsum(padded)
    pstart = pend - padded
    onehot_start = jnp.where(eidx_w[:, :TOP_K, None] == experts, pstart, 0)
    dest = (jnp.sum(onehot_start, axis=-1) + rank_w[:, :TOP_K]).astype(jnp.int32)
    n_blocks = (t * TOP_K) // bm + ne
    n_used = pend[-1] // bm
    blk = jnp.arange(n_blocks, dtype=jnp.int32)
    be = jnp.minimum(jnp.sum(pend[None, :] <= blk[:, None] * bm, axis=1), ne - 1).astype(jnp.int32)
    last_e = jnp.sum(jnp.where(blk == n_used - 1, be, 0))
    be = jnp.where(blk < n_used, be, last_e)
    pad_lo = jnp.concatenate([pstart + cnt, pend[-1:]])
    pad_hi = jnp.concatenate([pend, jnp.full((1,), n_blocks * bm, pend.dtype)])
    pad_ranges = jnp.stack([pad_lo, pad_hi]).astype(jnp.int32)
    x_rows = _moe_scatter(x1_rows, dest, pad_ranges, n_blocks * bm, tt=tt_scatter)
    y_rows, ys = _moe_ffn(x_rows, xs1.astype(BF16), be, n_used.reshape(1).astype(jnp.int32),
                          p["moe_w_gu"], p["moe_b_gu4"], p["moe_w_dn"], p["moe_b_dn4"], layer, bm=bm)
    ln_g, ln_b = p["ln_ffn_g"][layer][None], p["ln_ffn_b"][layer][None]
    out_big = _moe_combine(y_rows, dest, gate_w, x1, ln_g, ln_b, tt=tt_combine)
    out_small = _moe_dense_combine(ys, gate_dense, xs1, ln_g, ln_b)
    return out_big, out_small


def _latent_kernel(x_ref, wc_ref, g_ref, wr_ref, wrs_ref, cos_ref, sin_ref, *rest, with_kv):
    if with_kv:
        wuk_ref, wuv_ref, c_ref, kr_ref, kp_ref, vp_ref = rest
    else:
        c_ref, kr_ref = rest
    xb = x_ref[...].astype(BF16)
    kv = jnp.dot(xb, wc_ref[...], preferred_element_type=F32)
    c = _rms_norm(kv, g_ref[...])
    r = jnp.dot(xb, wr_ref[...], preferred_element_type=F32)
    rs = jnp.dot(xb, wrs_ref[...], preferred_element_type=F32)
    kr = r * cos_ref[...] + rs * sin_ref[...]
    c_ref[...] = c
    kr_ref[...] = kr
    if with_kv:
        cb = c.astype(BF16)
        kn = jnp.dot(cb, wuk_ref[...], preferred_element_type=F32)
        n_heads = kn.shape[1] // HEAD_PAD
        for h in range(n_heads):
            sl = slice(h * HEAD_PAD, (h + 1) * HEAD_PAD)
            kp_ref[:, sl] = (kn[:, sl] + kr).astype(BF16)
        vp_ref[...] = jnp.dot(cb, wuv_ref[...], preferred_element_type=F32).astype(BF16)


def _mla_latent(x, wc, g_kv, wr, wrs, cos_k, sin_k, wuk_p, wuv_p, *, tm, with_kv):
    t, d = x.shape
    kvl = wc.shape[1]
    n_tab = cos_k.shape[0] // tm
    in_specs = [
        pl.BlockSpec((tm, d), lambda i: (i, 0)),
        pl.BlockSpec((d, kvl), lambda i: (0, 0)),
        pl.BlockSpec((1, kvl), lambda i: (0, 0)),
        pl.BlockSpec((d, HEAD_PAD), lambda i: (0, 0)),
        pl.BlockSpec((d, HEAD_PAD), lambda i: (0, 0)),
        pl.BlockSpec((tm, HEAD_PAD), lambda i: (i % n_tab, 0)),
        pl.BlockSpec((tm, HEAD_PAD), lambda i: (i % n_tab, 0)),
    ]
    out_specs = [
        pl.BlockSpec((tm, kvl), lambda i: (i, 0)),
        pl.BlockSpec((tm, HEAD_PAD), lambda i: (i, 0)),
    ]
    out_shape = [jax.ShapeDtypeStruct((t, kvl), F32), jax.ShapeDtypeStruct((t, HEAD_PAD), F32)]
    args = [x, wc, g_kv, wr, wrs, cos_k, sin_k]
    if with_kv:
        hp = wuk_p.shape[1]
        in_specs += [pl.BlockSpec((kvl, hp), lambda i: (0, 0)), pl.BlockSpec((kvl, hp), lambda i: (0, 0))]
        out_specs += [pl.BlockSpec((tm, hp), lambda i: (i, 0)), pl.BlockSpec((tm, hp), lambda i: (i, 0))]
        out_shape += [jax.ShapeDtypeStruct((t, hp), BF16), jax.ShapeDtypeStruct((t, hp), BF16)]
        args += [wuk_p, wuv_p]
    return pl.pallas_call(
        functools.partial(_latent_kernel, with_kv=with_kv),
        grid=(t // tm,),
        in_specs=in_specs,
        out_specs=out_specs,
        out_shape=out_shape,
        compiler_params=_params(("arbitrary",)),
        name="mla_latent",
    )(*args)


def _query_kernel(x_ref, wqa_ref, g_ref, wqb_ref, wqs_ref, cos_ref, sin_ref, q_ref):
    xb = x_ref[...].astype(BF16)
    cq = _rms_norm(jnp.dot(xb, wqa_ref[...], preferred_element_type=F32), g_ref[...]).astype(BF16)
    q = jnp.dot(cq, wqb_ref[...], preferred_element_type=F32)
    qs = jnp.dot(cq, wqs_ref[...], preferred_element_type=F32)
    cos = cos_ref[...]
    sin = sin_ref[...]
    n_heads = q.shape[1] // HEAD_PAD
    for h in range(n_heads):
        sl = slice(h * HEAD_PAD, (h + 1) * HEAD_PAD)
        q_ref[:, sl] = (q[:, sl] * cos + qs[:, sl] * sin).astype(BF16)


def _mla_queries(x, wqa, g_q, wqb_p, wqb_s, cos_q, sin_q, *, tm):
    t, d = x.shape
    ql = wqa.shape[1]
    hp = wqb_p.shape[1]
    n_tab = cos_q.shape[0] // tm
    return pl.pallas_call(
        _query_kernel,
        grid=(t // tm,),
        in_specs=[
            pl.BlockSpec((tm, d), lambda i: (i, 0)),
            pl.BlockSpec((d, ql), lambda i: (0, 0)),
            pl.BlockSpec((1, ql), lambda i: (0, 0)),
            pl.BlockSpec((ql, hp), lambda i: (0, 0)),
            pl.BlockSpec((ql, hp), lambda i: (0, 0)),
            pl.BlockSpec((tm, HEAD_PAD), lambda i: (i % n_tab, 0)),
            pl.BlockSpec((tm, HEAD_PAD), lambda i: (i % n_tab, 0)),
        ],
        out_specs=pl.BlockSpec((tm, hp), lambda i: (i, 0)),
        out_shape=jax.ShapeDtypeStruct((t, hp), BF16),
        compiler_params=_params(("arbitrary",)),
        name="mla_queries",
    )(x, wqa, g_q, wqb_p, wqb_s, cos_q, sin_q)


def _flash_kernel(q_ref, k_ref, v_ref, o_ref, *, tq):
    seq = q_ref.shape[0]
    n_tiles = seq // tq
    row = lax.broadcasted_iota(jnp.int32, (tq, tq), 0)
    col = lax.broadcasted_iota(jnp.int32, (tq, tq), 1)
    for i in range(n_tiles):
        q = q_ref[i * tq:(i + 1) * tq, :]
        m = jnp.full((tq, 1), -jnp.inf, F32)
        l = jnp.zeros((tq, 1), F32)
        acc = jnp.zeros((tq, HEAD_PAD), F32)
        for j in range(i + 1):
            k = k_ref[j * tq:(j + 1) * tq, :]
            v = v_ref[j * tq:(j + 1) * tq, :]
            s = lax.dot_general(q, k, (((1,), (1,)), ((), ())), preferred_element_type=F32)
            if j == i:
                s = jnp.where(col <= row, s, -jnp.inf)
            m_new = jnp.maximum(m, jnp.max(s, axis=1, keepdims=True))
            alpha = jnp.exp2((m - m_new) * SCALE_LOG2E)
            pr = jnp.exp2((s - m_new) * SCALE_LOG2E)
            l = alpha * l + jnp.sum(pr, axis=1, keepdims=True)
            acc = alpha * acc + jnp.dot(pr.astype(BF16), v, preferred_element_type=F32)
            m = m_new
        o_ref[i * tq:(i + 1) * tq, :] = (acc / l).astype(o_ref.dtype)


def _flash_attention(qp, kp, vp, *, n_seq, tq):
    t, hp = qp.shape
    seq = t // n_seq
    n_heads = hp // HEAD_PAD
    q3 = qp.reshape(n_seq, seq, hp)
    k3 = kp.reshape(n_seq, seq, hp)
    v3 = vp.reshape(n_seq, seq, hp)
    spec = pl.BlockSpec((None, seq, HEAD_PAD), lambda n, h: (n, 0, h))
    out = pl.pallas_call(
        functools.partial(_flash_kernel, tq=tq),
        grid=(n_seq, n_heads),
        in_specs=[spec, spec, spec],
        out_specs=spec,
        out_shape=jax.ShapeDtypeStruct((n_seq, seq, hp), BF16),
        compiler_params=_params(("arbitrary", "arbitrary")),
        name="flash_attention",
    )(q3, k3, v3)
    return out.reshape(t, hp)


def _qlat_kernel(q_ref, m_ref, o_ref):
    o_ref[...] = jnp.dot(q_ref[...], m_ref[...], preferred_element_type=F32).astype(o_ref.dtype)


def _q_latent(qp, m_heads):
    t, hp = qp.shape
    n_heads, _, width = m_heads.shape
    return pl.pallas_call(
        _qlat_kernel,
        grid=(n_heads,),
        in_specs=[
            pl.BlockSpec((t, HEAD_PAD), lambda h: (0, h)),
            pl.BlockSpec((None, HEAD_PAD, width), lambda h: (h, 0, 0)),
        ],
        out_specs=pl.BlockSpec((None, t, width), lambda h: (h, 0, 0)),
        out_shape=jax.ShapeDtypeStruct((n_heads, t, width), BF16),
        compiler_params=_params(("arbitrary",)),
        name="q_latent",
    )(qp, m_heads)


def _paged_kernel(pt_ref, q_ref, cnew_ref, rnew_ref, ckv_hbm, kr_hbm, o_ref,
                  cbuf, rbuf, sem_c, sem_r, m_scr, l_scr, acc_scr,
                  *, n_chunks, pages_per_step, n_pages, dec_seq, kvl):
    s = pl.program_id(0)
    n_steps = pl.num_programs(0)
    j = s % n_chunks
    page = cbuf.shape[2]

    def copies(step, slot, p):
        pg = pt_ref[(step // n_chunks) * n_pages + (step % n_chunks) * pages_per_step + p]
        return (pltpu.make_async_copy(ckv_hbm.at[pg], cbuf.at[slot, p], sem_c.at[slot]),
                pltpu.make_async_copy(kr_hbm.at[pg], rbuf.at[slot, p], sem_r.at[slot]))

    def issue(step, slot):
        for p in range(pages_per_step):
            cc, cr = copies(step, slot, p)
            cc.start()
            cr.start()

    @pl.when(s == 0)
    def _():
        issue(0, 0)

    @pl.when(s + 1 < n_steps)
    def _():
        issue(s + 1, (s + 1) % 2)

    slot = s % 2
    for p in range(pages_per_step):
        cc, cr = copies(s, slot, p)
        cc.wait()
        cr.wait()

    @pl.when(j == 0)
    def _():
        m_scr[...] = jnp.full_like(m_scr, -jnp.inf)
        l_scr[...] = jnp.zeros_like(l_scr)
        acc_scr[...] = jnp.zeros_like(acc_scr)

    q = q_ref[...]
    q_lat = q[:, :kvl]
    q_pe = q[:, kvl:kvl + QK_ROPE]
    nt = (((1,), (1,)), ((), ()))

    def update(c_b, r_b, mask):
        sc = (lax.dot_general(q_lat, c_b, nt, preferred_element_type=F32)
              + lax.dot_general(q_pe, r_b, nt, preferred_element_type=F32))
        if mask is not None:
            sc = jnp.where(mask, sc, -jnp.inf)
        m_old = m_scr[...]
        m_new = jnp.maximum(m_old, jnp.max(sc, axis=1, keepdims=True))
        alpha = jnp.exp2((m_old - m_new) * SCALE_LOG2E)
        pr = jnp.exp2((sc - m_new) * SCALE_LOG2E)
        l_scr[...] = alpha * l_scr[...] + jnp.sum(pr, axis=1, keepdims=True)
        acc_scr[...] = alpha * acc_scr[...] + jnp.dot(pr.astype(BF16), c_b, preferred_element_type=F32)
        m_scr[...] = m_new

    c_b = cbuf[slot].reshape(pages_per_step * page, kvl).astype(BF16)
    r_b = rbuf[slot].reshape(pages_per_step * page, QK_ROPE).astype(BF16)
    update(c_b, r_b, None)

    @pl.when(j == n_chunks - 1)
    def _():
        rows = q.shape[0]
        n_new = cnew_ref.shape[0]
        q_l = lax.broadcasted_iota(jnp.int32, (rows, n_new), 0) % dec_seq
        kk = lax.broadcasted_iota(jnp.int32, (rows, n_new), 1)
        update(cnew_ref[...].astype(BF16), rnew_ref[...].astype(BF16), kk <= q_l)
        o_ref[...] = acc_scr[...] / l_scr[...]


def _paged_attention(qcat, c_new, kr_new, cache_ckv, cache_krope, page_table, *, dec_seq, n_heads,
                     pages_per_step):
    n_dec, n_pages = page_table.shape
    page, kvl = cache_ckv.shape[1:]
    rows = n_heads * dec_seq
    width = qcat.shape[1]
    n_chunks = n_pages // pages_per_step
    n_new = c_new.shape[1]
    kern = functools.partial(_paged_kernel, n_chunks=n_chunks, pages_per_step=pages_per_step,
                             n_pages=n_pages, dec_seq=dec_seq, kvl=kvl)
    grid_spec = pltpu.PrefetchScalarGridSpec(
        num_scalar_prefetch=1,
        grid=(n_dec * n_chunks,),
        in_specs=[
            pl.BlockSpec((rows, width), lambda s, pt: (s // n_chunks, 0)),
            pl.BlockSpec((None, n_new, kvl), lambda s, pt: (s // n_chunks, 0, 0)),
            pl.BlockSpec((None, n_new, QK_ROPE), lambda s, pt: (s // n_chunks, 0, 0)),
            pl.BlockSpec(memory_space=pl.ANY),
            pl.BlockSpec(memory_space=pl.ANY),
        ],
        out_specs=pl.BlockSpec((rows, kvl), lambda s, pt: (s // n_chunks, 0)),
        scratch_shapes=[
            pltpu.VMEM((2, pages_per_step, page, kvl), F32),
            pltpu.VMEM((2, pages_per_step, page, QK_ROPE), F32),
            pltpu.SemaphoreType.DMA((2,)),
            pltpu.SemaphoreType.DMA((2,)),
            pltpu.VMEM((rows, 1), F32),
            pltpu.VMEM((rows, 1), F32),
            pltpu.VMEM((rows, kvl), F32),
        ],
    )
    return pl.pallas_call(
        kern,
        grid_spec=grid_spec,
        out_shape=jax.ShapeDtypeStruct((n_dec * rows, kvl), F32),
        compiler_params=_params(("arbitrary",)),
        name="paged_attention",
    )(page_table.reshape(-1), qcat, c_new, kr_new, cache_ckv, cache_krope)


def _ov_kernel(o_ref, w_ref, out_ref):
    out_ref[...] = jnp.dot(o_ref[...].astype(BF16), w_ref[...], preferred_element_type=F32)


def _value_up(o_lat_hm, w_uv_hm):
    n_heads, t, kvl = o_lat_hm.shape
    vh = w_uv_hm.shape[2]
    return pl.pallas_call(
        _ov_kernel,
        grid=(n_heads,),
        in_specs=[
            pl.BlockSpec((None, t, kvl), lambda h: (h, 0, 0)),
            pl.BlockSpec((None, kvl, vh), lambda h: (h, 0, 0)),
        ],
        out_specs=pl.BlockSpec((None, t, vh), lambda h: (h, 0, 0)),
        out_shape=jax.ShapeDtypeStruct((n_heads, t, vh), F32),
        compiler_params=_params(("arbitrary",)),
        name="value_up",
    )(o_lat_hm, w_uv_hm)


def _s5_weights(lam_re, lam_im, log_dt, b_re, b_im, c_re, c_im):
    lr = lam_re.astype(F32)
    li = lam_im.astype(F32)
    dt = jnp.exp(log_dt.astype(F32))[:, None]
    mag = jnp.exp(lr * dt)
    ang = li * dt
    ab_re = mag * jnp.cos(ang)
    ab_im = mag * jnp.sin(ang)
    den = lr * lr + li * li
    f_re = ((ab_re - 1.0) * lr + ab_im * li) / den
    f_im = (ab_im * lr - (ab_re - 1.0) * li) / den
    br = b_re.astype(F32)
    bi = b_im.astype(F32)
    bb_re = f_re[..., None] * br - f_im[..., None] * bi
    bb_im = f_re[..., None] * bi + f_im[..., None] * br
    g = lr.shape[0]
    gpc = S5_GROUPS_PER_CHUNK
    n_chunks = g // gpc
    eye = jnp.eye(gpc, dtype=F32)

    def b_block(bb):
        t = bb.reshape(n_chunks, gpc, SSM_STATE, SSM_GROUP).transpose(0, 1, 3, 2)
        blk = t[:, :, :, None, :] * eye[None, :, None, :, None]
        return blk.reshape(n_chunks, gpc * SSM_GROUP, gpc * SSM_STATE)

    def c_block(cc):
        t = cc.reshape(n_chunks, gpc, SSM_GROUP, SSM_STATE).transpose(0, 1, 3, 2)
        blk = t[:, :, :, None, :] * eye[None, :, None, :, None]
        return blk.reshape(n_chunks, gpc * SSM_STATE, gpc * SSM_GROUP)

    bblk = jnp.concatenate([b_block(bb_re), b_block(bb_im)], axis=2).astype(BF16)
    cblk = jnp.concatenate([c_block(c_re.astype(F32)), -c_block(c_im.astype(F32))], axis=1).astype(BF16)
    a_re = ab_re.reshape(n_chunks, 1, gpc * SSM_STATE)
    a_im = ab_im.reshape(n_chunks, 1, gpc * SSM_STATE)
    return bblk, cblk, a_re, a_im


def _rope_tables(pos, n_rep):
    inv = 1.0 / (ROPE_THETA ** (jnp.arange(0, QK_ROPE, 2, dtype=F32) / QK_ROPE))
    ang = pos.astype(F32)[:, None] * inv[None, :]
    cos = jnp.cos(ang)
    sin = jnp.sin(ang)
    length = pos.shape[0]
    zero_lo = jnp.zeros((length, QK_NOPE), F32)
    zero_hi = jnp.zeros((length, HEAD_PAD - QK_NOPE - QK_ROPE), F32)
    cos_k = jnp.concatenate([zero_lo, cos, cos, zero_hi], axis=1)
    sin_k = jnp.concatenate([zero_lo, sin, sin, zero_hi], axis=1)
    cos_q = jnp.concatenate([jnp.ones((length, QK_NOPE), F32), cos, cos, zero_hi], axis=1)
    if n_rep > 1:
        cos_k, sin_k, cos_q = (jnp.tile(a, (n_rep, 1)) for a in (cos_k, sin_k, cos_q))
    return cos_k, sin_k, cos_q


def _pad_heads(w, n_heads, width, offset=0):
    k = w.shape[0]
    w3 = w.reshape(k, n_heads, width)
    out = jnp.zeros((k, n_heads, HEAD_PAD), w.dtype).at[:, :, offset:offset + width].set(w3)
    return out.reshape(k, n_heads * HEAD_PAD)


def _swap_rope(w_rope):
    half = QK_ROPE // 2
    return jnp.concatenate([-w_rope[..., half:], w_rope[..., :half]], axis=-1)


def _mla_weights(p, b):
    n_heads = p["mla_w_uk"].shape[1]
    kvl = p["mla_w_uk"].shape[0]
    w_qb = p["mla_w_q_b"][b]
    ql = w_qb.shape[0]
    w3 = w_qb.reshape(ql, n_heads, QK_NOPE + QK_ROPE)
    zero_tail = jnp.zeros((ql, n_heads, HEAD_PAD - QK_NOPE - QK_ROPE), F32)
    wqb_p = jnp.concatenate([w3, zero_tail], axis=2).reshape(ql, n_heads * HEAD_PAD)
    wqb_s = jnp.concatenate([jnp.zeros((ql, n_heads, QK_NOPE), F32), _swap_rope(w3[..., QK_NOPE:]),
                             zero_tail], axis=2).reshape(ql, n_heads * HEAD_PAD)
    w_o = p["mla_w_o"][b]
    d_model = w_o.shape[1]
    w_o_p = jnp.zeros((n_heads, HEAD_PAD, d_model), F32).at[:, :V_HEAD].set(
        w_o.reshape(n_heads, V_HEAD, d_model)).reshape(n_heads * HEAD_PAD, d_model)
    del kvl
    return dict(wqa=p["mla_w_q_a"][b].astype(BF16), g_q=p["mla_g_q"][b][None],
                wqb_p=wqb_p.astype(BF16), wqb_s=wqb_s.astype(BF16),
                w_o=w_o.astype(BF16), w_o_p=w_o_p.astype(BF16))


def _shared_mla_weights(p):
    w_kv_a = p["mla_w_kv_a"]
    d_model = w_kv_a.shape[0]
    kvl, n_heads, _ = p["mla_w_uk"].shape
    w_rope = w_kv_a[:, kvl:]
    pad_lo = jnp.zeros((d_model, QK_NOPE), F32)
    pad_hi = jnp.zeros((d_model, HEAD_PAD - QK_NOPE - QK_ROPE), F32)
    wr = jnp.concatenate([pad_lo, w_rope, pad_hi], axis=1)
    wrs = jnp.concatenate([pad_lo, _swap_rope(w_rope), pad_hi], axis=1)
    wuk_p = _pad_heads(p["mla_w_uk"].reshape(kvl, n_heads * QK_NOPE), n_heads, QK_NOPE)
    wuv_p = _pad_heads(p["mla_w_uv"].reshape(kvl, n_heads * V_HEAD), n_heads, V_HEAD)
    width = kvl + LANES
    m_heads = jnp.zeros((n_heads, HEAD_PAD, width), F32)
    m_heads = m_heads.at[:, :QK_NOPE, :kvl].set(p["mla_w_uk"].transpose(1, 2, 0))
    m_heads = m_heads.at[:, QK_NOPE:QK_NOPE + QK_ROPE, kvl:kvl + QK_ROPE].set(
        jnp.broadcast_to(jnp.eye(QK_ROPE, dtype=F32), (n_heads, QK_ROPE, QK_ROPE)))
    return dict(wc=w_kv_a[:, :kvl].astype(BF16), g_kv=p["mla_g_kv"][None], wr=wr.astype(BF16),
                wrs=wrs.astype(BF16), wuk_p=wuk_p.astype(BF16), wuv_p=wuv_p.astype(BF16),
                m_heads=m_heads.astype(BF16), w_uv_hm=p["mla_w_uv"].transpose(1, 0, 2).astype(BF16))


def _tiles(t):
    big = t >= 4096
    return dict(tm=512 if big else min(t, 128), bm=512 if big else 64,
                tt_route=512 if big else min(t, 128), tt_scatter=512 if big else min(t, 128),
                tt_combine=256 if big else min(t, 128))


def _trunk(x, pos, h0, past, p, s5w, mla_shared, mla_layers):
    n_seq, seq, d_model = x.shape
    t = n_seq * seq
    cfg = _tiles(t)
    n_a = len(s5w)
    n_states = d_model // SSM_GROUP * SSM_STATE

    assert d_model == ROW_TILE * LANES, "MoE row buffers hold one (8, 128) tile per row"

    xt = x.transpose(1, 0, 2).reshape(t, d_model)
    lt = min(seq, 256)
    new_re, new_im = [], []
    for a in range(n_a):
        bblk, cblk, a_re, a_im = s5w[a]
        if h0 is None:
            h0r = jnp.zeros((n_seq, n_states), F32)
            h0i = jnp.zeros((n_seq, n_states), F32)
        else:
            h0r = h0[0][a].reshape(n_seq, n_states)
            h0i = h0[1][a].reshape(n_seq, n_states)
        z, hr, hi = _s5_scan(xt, bblk, cblk, a_re, a_im, p["ssm_d"][a][None], h0r, h0i,
                             n_seq=n_seq, lt=lt)
        new_re.append(hr.reshape(n_seq, d_model // SSM_GROUP, SSM_STATE))
        new_im.append(hi.reshape(n_seq, d_model // SSM_GROUP, SSM_STATE))
        proj = _proj_ln_router(z, p["ssm_w_glu_bf"][a], p["ssm_b_glu"][a][None], xt,
                               p["ln_mix_g"][a][None], p["ln_mix_b"][a][None],
                               p["moe_w_router"][a], p["moe_b_router"][a][None],
                               glu=True, tm=cfg["tm"])
        xt = yield proj

    xb = xt.reshape(seq, n_seq, d_model).transpose(1, 0, 2).reshape(t, d_model)
    tm = cfg["tm"]
    n_rep = 1 if seq >= tm else tm // seq
    cos_k, sin_k, cos_q = _rope_tables(pos, n_rep)
    prompt = past is None
    lat = _mla_latent(xb, mla_shared["wc"], mla_shared["g_kv"], mla_shared["wr"], mla_shared["wrs"],
                      cos_k, sin_k, mla_shared["wuk_p"], mla_shared["wuv_p"], tm=tm, with_kv=prompt)
    c_lat, kr128 = lat[0], lat[1]
    kr = kr128[:, QK_NOPE:QK_NOPE + QK_ROPE]
    kvl = c_lat.shape[1]
    n_heads = mla_shared["m_heads"].shape[0]
    zero_bias = jnp.zeros((1, d_model), F32)
    if not prompt:
        cache_ckv, cache_krope, page_table = past
        n_new = -(-seq // SUBLANES) * SUBLANES
        c_new = jnp.zeros((n_seq, n_new, kvl), F32).at[:, :seq].set(c_lat.reshape(n_seq, seq, kvl))
        kr_new = jnp.zeros((n_seq, n_new, QK_ROPE), F32).at[:, :seq].set(kr.reshape(n_seq, seq, QK_ROPE))
    for b, mw in enumerate(mla_layers):
        layer = n_a + b
        qp = _mla_queries(xb, mw["wqa"], mw["g_q"], mw["wqb_p"], mw["wqb_s"], cos_q, sin_k, tm=tm)
        if prompt:
            o = _flash_attention(qp, lat[2], lat[3], n_seq=n_seq, tq=min(seq, 512))
            w_o = mw["w_o_p"]
        else:
            qcat = _q_latent(qp, mla_shared["m_heads"])
            width = qcat.shape[2]
            qcat = qcat.reshape(n_heads, n_seq, seq, width).transpose(1, 0, 2, 3).reshape(
                n_seq * n_heads * seq, width)
            o_lat = _paged_attention(qcat, c_new, kr_new, cache_ckv, cache_krope, page_table,
                                     dec_seq=seq, n_heads=n_heads,
                                     pages_per_step=math.gcd(page_table.shape[1], PAGES_PER_STEP))
            o_hm = o_lat.reshape(n_seq, n_heads, seq, kvl).transpose(1, 0, 2, 3).reshape(n_heads, t, kvl)
            o = _value_up(o_hm, mla_shared["w_uv_hm"])
            o = o.transpose(1, 0, 2).reshape(t, n_heads * V_HEAD)
            w_o = mw["w_o"]
        proj = _proj_ln_router(o, w_o, zero_bias, xb, p["ln_mix_g"][layer][None],
                               p["ln_mix_b"][layer][None], p["moe_w_router"][layer],
                               p["moe_b_router"][layer][None], glu=False, tm=tm)
        xb = yield proj
    y = xb.reshape(n_seq, seq, d_model)
    return (y, jnp.stack(new_re), jnp.stack(new_im), c_lat.reshape(n_seq, seq, kvl),
            kr.reshape(n_seq, seq, QK_ROPE))


def kernel(x_prompt, x_sample, state_ssm_re, state_ssm_im, cache_ckv, cache_krope, page_table,
           ssm_lam_re, ssm_lam_im, ssm_log_dt, ssm_b_re, ssm_b_im, ssm_c_re, ssm_c_im, ssm_d,
           ssm_w_glu, ssm_b_glu, mla_w_kv_a, mla_g_kv, mla_w_uk, mla_w_uv, mla_w_q_a, mla_g_q,
           mla_w_q_b, mla_w_o, moe_w_router, moe_b_router, moe_w_gu, moe_b_gu, moe_w_dn, moe_b_dn,
           ln_mix_g, ln_mix_b, ln_ffn_g, ln_ffn_b):
    p = dict(ssm_d=ssm_d, ssm_b_glu=ssm_b_glu, ssm_w_glu_bf=ssm_w_glu.astype(BF16),
             mla_w_kv_a=mla_w_kv_a, mla_g_kv=mla_g_kv, mla_w_uk=mla_w_uk, mla_w_uv=mla_w_uv,
             mla_w_q_a=mla_w_q_a, mla_g_q=mla_g_q, mla_w_q_b=mla_w_q_b, mla_w_o=mla_w_o,
             moe_w_router=moe_w_router, moe_b_router=moe_b_router, moe_w_gu=moe_w_gu,
             moe_b_gu4=moe_b_gu[:, :, None, :], moe_w_dn=moe_w_dn, moe_b_dn4=moe_b_dn[:, :, None, :],
             ln_mix_g=ln_mix_g, ln_mix_b=ln_mix_b, ln_ffn_g=ln_ffn_g, ln_ffn_b=ln_ffn_b)
    n_a = ssm_lam_re.shape[0]
    s5w = [_s5_weights(ssm_lam_re[a], ssm_lam_im[a], ssm_log_dt[a], ssm_b_re[a], ssm_b_im[a],
                       ssm_c_re[a], ssm_c_im[a]) for a in range(n_a)]
    mla_shared = _shared_mla_weights(p)
    mla_layers = [_mla_weights(p, b) for b in range(mla_w_q_a.shape[0])]

    n_pages = page_table.shape[1]
    past_len = n_pages * cache_ckv.shape[1]
    pos_sample = past_len + jnp.arange(x_sample.shape[1])
    stream_s = _trunk(x_sample, pos_sample, (state_ssm_re, state_ssm_im),
                      (cache_ckv, cache_krope, page_table), p, s5w, mla_shared, mla_layers)
    pos_prompt = jnp.arange(x_prompt.shape[1])
    stream_p = _trunk(x_prompt, pos_prompt, None, None, p, s5w, mla_shared, mla_layers)
    cfg = _tiles(x_prompt.shape[0] * x_prompt.shape[1])
    tt_small = min(x_sample.shape[0] * x_sample.shape[1], 128)
    req_s, req_p = next(stream_s), next(stream_p)
    out_s = out_p = None
    for layer in range(moe_w_gu.shape[0]):
        new_p, new_s = _moe_pair(req_p, req_s, p, layer, bm=cfg["bm"], tt_route=cfg["tt_route"],
                                 tt_scatter=cfg["tt_scatter"], tt_combine=cfg["tt_combine"],
                                 tt_small=tt_small)
        try:
            req_s = stream_s.send(new_s)
        except StopIteration as done:
            out_s = done.value
        try:
            req_p = stream_p.send(new_p)
        except StopIteration as done:
            out_p = done.value
    return (out_p[0], out_s[0], out_p[1], out_p[2], out_p[3], out_p[4],
            out_s[1], out_s[2], out_s[3], out_s[4])
```
